```python
import math
import jax, jax.numpy as jnp
from jax import lax
import numpy as np

D_MODEL = 1024
BATCH = 8
SEQ = 4096
DEPTH = 4

MIX_WIDTH = D_MODEL
FOURIER_WIDTH = MIX_WIDTH // 2
ATTN_WIDTH = MIX_WIDTH - FOURIER_WIDTH
N_FOURIER_GROUPS = 4
FOURIER_GROUP_DIM = FOURIER_WIDTH // N_FOURIER_GROUPS
N_ATTN_HEADS = 4
HEAD_DIM = ATTN_WIDTH // (2 * N_ATTN_HEADS)
V_HEAD_DIM = 2 * HEAD_DIM
QK_WIDTH = N_ATTN_HEADS * 2 * HEAD_DIM
V_WIDTH = N_ATTN_HEADS * V_HEAD_DIM
IN_WIDTH = FOURIER_WIDTH + 2 * QK_WIDTH + V_WIDTH + MIX_WIDTH
ROT_DIM = HEAD_DIM // 4
ROPE_THETA = 500000.0
N_META = 16
Q_BLOCK = 128
NORM_EPS = 1e-6

kernel_name = "hybrid_fourier_diffattn_encoder"


def rms_norm(x, g, eps=NORM_EPS):
    xf = x.astype(jnp.float32)
    y = xf * lax.rsqrt(jnp.mean(xf * xf, axis=-1, keepdims=True) + eps)
    return y.astype(x.dtype) * g


def rope_tables(length):
    inv_freq = ROPE_THETA ** (-jnp.arange(0, ROT_DIM, 2, dtype=jnp.float32) / ROT_DIM)
    pos = jnp.arange(length, dtype=jnp.float32)
    ang = pos[:, None] * inv_freq[None, :]
    return jnp.cos(ang), jnp.sin(ang)


def apply_partial_rotary(x, cos, sin):
    xr = x[..., :ROT_DIM].astype(jnp.float32)
    half = ROT_DIM // 2
    x1, x2 = xr[..., :half], xr[..., half:]
    c = cos[None, :, None, None, :]
    s = sin[None, :, None, None, :]
    rot = jnp.concatenate([x1 * c - x2 * s, x2 * c + x1 * s], axis=-1)
    return jnp.concatenate([rot.astype(x.dtype), x[..., ROT_DIM:]], axis=-1)


def fourier_mixer(f_in, w_f):
    b, l, _ = f_in.shape
    u = f_in.reshape(b, l, N_FOURIER_GROUPS, FOURIER_GROUP_DIM).astype(jnp.float32)
    f = jnp.fft.fft2(u, axes=(1, 3), norm="ortho").real.astype(f_in.dtype)
    out = jnp.einsum('blgc,gce->blge', f, w_f)
    return out.reshape(b, l, FOURIER_WIDTH)


def diff_attention(q, k, v, lam):
    b, l = q.shape[0], q.shape[1]
    seq = l - N_META
    scale = HEAD_DIM ** -0.5
    vf = v.astype(jnp.float32)
    lam_f = lam.astype(jnp.float32)

    def attend(qb):
        s = jnp.einsum('bqhcd,bkhcd->bhcqk', qb, k).astype(jnp.float32) * scale
        p = jax.nn.softmax(s, axis=-1)
        a = p[:, :, 0] - lam_f * p[:, :, 1]
        o = jnp.einsum('bhqk,bkhd->bqhd', a, vf)
        return o.astype(v.dtype)

    o_meta = attend(q[:, :N_META])
    nb = seq // Q_BLOCK
    q_real = q[:, N_META:].reshape(b, nb, Q_BLOCK, N_ATTN_HEADS, 2, HEAD_DIM)
    o_real = lax.map(attend, jnp.moveaxis(q_real, 1, 0))
    o_real = jnp.moveaxis(o_real, 0, 1).reshape(b, seq, N_ATTN_HEADS, V_HEAD_DIM)
    return jnp.concatenate([o_meta, o_real], axis=1)


def setup_inputs(seed: int = 0) -> dict:
    key = jax.random.key(seed)
    ks = jax.random.split(key, 16)
    f32 = jnp.float32
    x = jax.random.normal(ks[0], (BATCH, SEQ, D_MODEL), f32)
    meta_tokens = jax.random.normal(ks[1], (N_META, D_MODEL), f32)
    norm_gain = 1.0 + 0.02 * jax.random.normal(ks[2], (DEPTH, D_MODEL), f32)
    w_in = jax.random.normal(ks[3], (DEPTH, D_MODEL, IN_WIDTH), f32) * D_MODEL ** -0.5
    w_fourier = jax.random.normal(ks[4], (DEPTH, N_FOURIER_GROUPS, FOURIER_GROUP_DIM, FOURIER_GROUP_DIM), f32) * FOURIER_GROUP_DIM ** -0.5
    q_norm_gain = 1.0 + 0.02 * jax.random.normal(ks[5], (DEPTH, HEAD_DIM), f32)
    k_norm_gain = 1.0 + 0.02 * jax.random.normal(ks[6], (DEPTH, HEAD_DIM), f32)
    lambda_q1 = 0.1 * jax.random.normal(ks[7], (DEPTH, HEAD_DIM), f32)
    lambda_k1 = 0.1 * jax.random.normal(ks[8], (DEPTH, HEAD_DIM), f32)
    lambda_q2 = 0.1 * jax.random.normal(ks[9], (DEPTH, HEAD_DIM), f32)
    lambda_k2 = 0.1 * jax.random.normal(ks[10], (DEPTH, HEAD_DIM), f32)
    subln_gain = 1.0 + 0.02 * jax.random.normal(ks[11], (DEPTH, V_HEAD_DIM), f32)
    w_out = jax.random.normal(ks[12], (DEPTH, MIX_WIDTH, D_MODEL), f32) * MIX_WIDTH ** -0.5
    return {"x": x, "meta_tokens": meta_tokens, "norm_gain": norm_gain, "w_in": w_in,
            "w_fourier": w_fourier, "q_norm_gain": q_norm_gain, "k_norm_gain": k_norm_gain,
            "lambda_q1": lambda_q1, "lambda_k1": lambda_k1, "lambda_q2": lambda_q2,
            "lambda_k2": lambda_k2, "subln_gain": subln_gain, "w_out": w_out}


def reference(x, meta_tokens, norm_gain, w_in, w_fourier, q_norm_gain, k_norm_gain,
              lambda_q1, lambda_k1, lambda_q2, lambda_k2, subln_gain, w_out):
    b = x.shape[0]
    meta = jnp.broadcast_to(meta_tokens[None].astype(x.dtype), (b, N_META, D_MODEL))
    h_res = jnp.concatenate([meta, x], axis=1)
    l = h_res.shape[1]
    cos, sin = rope_tables(l)
    splits = [FOURIER_WIDTH, FOURIER_WIDTH + QK_WIDTH, FOURIER_WIDTH + 2 * QK_WIDTH,
              FOURIER_WIDTH + 2 * QK_WIDTH + V_WIDTH]

    for li in range(DEPTH):
        lambda_init = 0.8 - 0.6 * math.exp(-0.3 * li)
        h = rms_norm(h_res, norm_gain[li])
        proj = h @ w_in[li]
        f_in, q, k, v, gate = jnp.split(proj, splits, axis=-1)

        f_out = fourier_mixer(f_in, w_fourier[li])

        q = q.reshape(b, l, N_ATTN_HEADS, 2, HEAD_DIM)
        k = k.reshape(b, l, N_ATTN_HEADS, 2, HEAD_DIM)
        v = v.reshape(b, l, N_ATTN_HEADS, V_HEAD_DIM)
        q = apply_partial_rotary(rms_norm(q, q_norm_gain[li]), cos, sin)
        k = apply_partial_rotary(rms_norm(k, k_norm_gain[li]), cos, sin)
        lam = (jnp.exp(jnp.sum(lambda_q1[li].astype(jnp.float32) * lambda_k1[li].astype(jnp.float32)))
               - jnp.exp(jnp.sum(lambda_q2[li].astype(jnp.float32) * lambda_k2[li].astype(jnp.float32)))
               + lambda_init)
        o = diff_attention(q, k, v, lam)
        o = rms_norm(o, subln_gain[li]) * (1.0 - lambda_init)
        a_out = o.reshape(b, l, V_WIDTH)

        y = jnp.concatenate([f_out, a_out], axis=-1) * jax.nn.silu(gate)
        h_res = h_res + y @ w_out[li]

    return h_res[:, N_META:]
```

```python
import functools
import math

import jax
import jax.numpy as jnp
from jax import lax
from jax.experimental import pallas as pl
from jax.experimental.pallas import tpu as pltpu

F32 = jnp.float32
BF16 = jnp.bfloat16

N_META = 16
N_FOURIER_GROUPS = 4
N_ATTN_HEADS = 4
ROPE_THETA = 500000.0
NORM_EPS = 1e-6

MXU_TILE = 256
LANES = 128
VMEM_LIMIT = 56 * 1024 * 1024

ROW_TILE = 544
DFT_TILE = 544
Q_TILE = 256


def _padded_len(l):
    return -(-l // MXU_TILE) * MXU_TILE


def _inproj_kernel(h_ref, g_ref, w_ref, qkg_ref, ct_ref, sa_ref, sb_ref, cdft_ref,
                   pc_ref, ps_ref, qk_ref, v_ref, sg_ref, *, fw, qkw, vw, hd, qscale):
    x = h_ref[...]
    ms = jnp.mean(x * x, axis=-1, keepdims=True)
    hn = ((x * lax.rsqrt(ms + NORM_EPS)) * g_ref[...]).astype(BF16)

    f = jnp.dot(hn, w_ref[:, 0:fw], preferred_element_type=F32)
    gd = fw // N_FOURIER_GROUPS
    for g in range(N_FOURIER_GROUPS):
        pg = jnp.dot(f[:, g * gd:(g + 1) * gd].astype(BF16), cdft_ref[...],
                     preferred_element_type=F32)
        pc_ref[:, g * gd:(g + 1) * gd] = pg[:, :gd].astype(BF16)
        ps_ref[:, g * gd:(g + 1) * gd] = pg[:, gd:].astype(BF16)

    qk = jnp.dot(hn, w_ref[:, fw:fw + 2 * qkw], preferred_element_type=F32)
    tm = qk.shape[0]
    lo_mask = lax.broadcasted_iota(jnp.int32, (tm, LANES), 1) < hd
    ct = ct_ref[...]
    sa = sa_ref[...]
    sb = sb_ref[...]
    for c in range(2 * qkw // LANES):
        xc = qk[:, c * LANES:(c + 1) * LANES]
        x2 = xc * xc
        lo = jnp.sum(jnp.where(lo_mask, x2, 0.0), axis=-1, keepdims=True)
        hi = jnp.sum(jnp.where(lo_mask, 0.0, x2), axis=-1, keepdims=True)
        msc = jnp.where(lo_mask, lo, hi) * (1.0 / hd)
        y = (xc * lax.rsqrt(msc + NORM_EPS)) * qkg_ref[:, c * LANES:(c + 1) * LANES]
        yr = (y * ct + pltpu.roll(y, LANES - 8, 1) * sa + pltpu.roll(y, 8, 1) * sb)
        if c < qkw // LANES:
            yr = yr * qscale
        qk_ref[:, c * LANES:(c + 1) * LANES] = yr.astype(BF16)

    v = jnp.dot(hn, w_ref[:, fw + 2 * qkw:fw + 2 * qkw + vw], preferred_element_type=F32)
    v_ref[...] = v.astype(BF16)

    gate = jnp.dot(hn, w_ref[:, fw + 2 * qkw + vw:], preferred_element_type=F32)
    sg_ref[...] = gate * (1.0 / (1.0 + jnp.exp(-gate)))


def _inproj(h, gain, w_bf, qk_gain, ct, sa, sb, cdft, *, l_pad, fw, qkw, vw, hd, qscale):
    rows, d = h.shape
    tm = ROW_TILE
    n_pos_tiles = l_pad // tm
    mixw = w_bf.shape[1] - (fw + 2 * qkw + vw)
    row_spec = lambda w: pl.BlockSpec((tm, w), lambda i: (i, 0))
    const = lambda shape: pl.BlockSpec(shape, lambda i: (0,) * len(shape))
    pos_spec = pl.BlockSpec((tm, LANES), lambda i: (i % n_pos_tiles, 0))
    return pl.pallas_call(
        functools.partial(_inproj_kernel, fw=fw, qkw=qkw, vw=vw, hd=hd, qscale=qscale),
        grid=(rows // tm,),
        in_specs=[row_spec(d), const((1, d)), const(w_bf.shape), const((1, 2 * qkw)),
                  pos_spec, pos_spec, pos_spec, const(cdft.shape)],
        out_specs=[row_spec(fw), row_spec(fw), row_spec(2 * qkw), row_spec(vw), row_spec(mixw)],
        out_shape=[jax.ShapeDtypeStruct((rows, fw), BF16),
                   jax.ShapeDtypeStruct((rows, fw), BF16),
                   jax.ShapeDtypeStruct((rows, 2 * qkw), BF16),
                   jax.ShapeDtypeStruct((rows, vw), BF16),
                   jax.ShapeDtypeStruct((rows, mixw), F32)],
        compiler_params=pltpu.CompilerParams(
            dimension_semantics=("arbitrary",), vmem_limit_bytes=VMEM_LIMIT),
        name="inproj",
    )(h, gain, w_bf, qk_gain, ct, sa, sb, cdft)


def _fourier_kernel(cm_ref, sm_ref, pc_ref, ps_ref, wf_ref, o_ref, *, scale):
    re = (jnp.dot(cm_ref[...], pc_ref[0], preferred_element_type=F32)
          + jnp.dot(sm_ref[...], ps_ref[0], preferred_element_type=F32)) * scale
    gd = re.shape[1] // N_FOURIER_GROUPS
    for g in range(N_FOURIER_GROUPS):
        o_ref[0, :, g * gd:(g + 1) * gd] = jnp.dot(
            re[:, g * gd:(g + 1) * gd].astype(BF16), wf_ref[g], preferred_element_type=F32)


def _fourier(cmat, smat, pc, ps, wf_bf, *, scale):
    b, l_pad, fw = pc.shape
    tk = DFT_TILE
    return pl.pallas_call(
        functools.partial(_fourier_kernel, scale=scale),
        grid=(l_pad // tk, b),
        in_specs=[pl.BlockSpec((tk, l_pad), lambda i, j: (i, 0)),
                  pl.BlockSpec((tk, l_pad), lambda i, j: (i, 0)),
                  pl.BlockSpec((1, l_pad, fw), lambda i, j: (j, 0, 0)),
                  pl.BlockSpec((1, l_pad, fw), lambda i, j: (j, 0, 0)),
                  pl.BlockSpec(wf_bf.shape, lambda i, j: (0, 0, 0))],
        out_specs=pl.BlockSpec((1, tk, fw), lambda i, j: (j, i, 0)),
        out_shape=jax.ShapeDtypeStruct((b, l_pad, fw), F32),
        compiler_params=pltpu.CompilerParams(
            dimension_semantics=("arbitrary", "arbitrary"), vmem_limit_bytes=VMEM_LIMIT),
        name="fourier",
    )(cmat, smat, pc, ps, wf_bf)


def _attn_kernel(q_ref, k_ref, v_ref, lq1_ref, lk1_ref, lq2_ref, lk2_ref, sub_ref, o_ref,
                 *, hd, n_valid, lambda_init):
    q = q_ref[0]
    k = k_ref[0]
    v = v_ref[0]
    l_pad = k.shape[0]
    tq = q.shape[0]
    n_main = (n_valid // MXU_TILE) * MXU_TILE
    lane = lax.broadcasted_iota(jnp.int32, q.shape, 1)
    tail_valid = (lax.broadcasted_iota(jnp.int32, (l_pad - n_main, tq), 0) + n_main) < n_valid
    zero = jnp.zeros_like(q)
    contract_last = (((1,), (1,)), ((), ()))
    contract_first = (((0,), (0,)), ((), ()))

    outs = []
    for c in range(2):
        qc = jnp.where((lane >= c * hd) & (lane < (c + 1) * hd), q, zero)
        s = lax.dot_general(k, qc, contract_last, preferred_element_type=F32)
        s_main = s[:n_main]
        s_tail = jnp.where(tail_valid, s[n_main:], -jnp.inf)
        m = jnp.maximum(jnp.max(s_main, axis=0, keepdims=True),
                        jnp.max(s_tail, axis=0, keepdims=True))
        p_main = jnp.exp2(s_main - m)
        p_tail = jnp.exp2(s_tail - m)
        l = jnp.sum(p_main, axis=0, keepdims=True) + jnp.sum(p_tail, axis=0, keepdims=True)
        o_t = (lax.dot_general(v[:n_main], p_main.astype(BF16), contract_first,
                               preferred_element_type=F32)
               + lax.dot_general(v[n_main:], p_tail.astype(BF16), contract_first,
                                 preferred_element_type=F32))
        outs.append(o_t * (1.0 / l))

    lam = (jnp.exp(jnp.sum(lq1_ref[...] * lk1_ref[...], axis=-1, keepdims=True))
           - jnp.exp(jnp.sum(lq2_ref[...] * lk2_ref[...], axis=-1, keepdims=True))
           + lambda_init)
    o = (outs[0] - lam * outs[1]).T
    ms = jnp.mean(o * o, axis=-1, keepdims=True)
    o_ref[0] = ((o * lax.rsqrt(ms + NORM_EPS)) * sub_ref[...]) * (1.0 - lambda_init)


def _attention(qk, v, lq1, lk1, lq2, lk2, sub_gain, *, hd, n_valid, lambda_init):
    b, l_pad, _ = qk.shape
    vd = 2 * hd
    tq = Q_TILE
    vec = lambda n: pl.BlockSpec((1, n), lambda bi, h, qi: (0, 0))
    return pl.pallas_call(
        functools.partial(_attn_kernel, hd=hd, n_valid=n_valid, lambda_init=lambda_init),
        grid=(b, N_ATTN_HEADS, l_pad // tq),
        in_specs=[pl.BlockSpec((1, tq, 2 * hd), lambda bi, h, qi: (bi, qi, h)),
                  pl.BlockSpec((1, l_pad, 2 * hd), lambda bi, h, qi: (bi, 0, N_ATTN_HEADS + h)),
                  pl.BlockSpec((1, l_pad, vd), lambda bi, h, qi: (bi, 0, h)),
                  vec(hd), vec(hd), vec(hd), vec(hd), vec(vd)],
        out_specs=pl.BlockSpec((1, tq, vd), lambda bi, h, qi: (bi, qi, h)),
        out_shape=jax.ShapeDtypeStruct((b, l_pad, N_ATTN_HEADS * vd), F32),
        compiler_params=pltpu.CompilerParams(
            dimension_semantics=("arbitrary", "arbitrary", "arbitrary"),
            vmem_limit_bytes=VMEM_LIMIT),
        name="diffattn",
    )(qk, qk, v, lq1, lk1, lq2, lk2, sub_gain)


def _outproj_kernel(h_ref, f_ref, a_ref, sg_ref, w_ref, o_ref):
    fw = f_ref.shape[1]
    sg = sg_ref[...]
    yf = (f_ref[...] * sg[:, :fw]).astype(BF16)
    ya = (a_ref[...] * sg[:, fw:]).astype(BF16)
    o_ref[...] = (h_ref[...]
                  + jnp.dot(yf, w_ref[:fw, :], preferred_element_type=F32)
                  + jnp.dot(ya, w_ref[fw:, :], preferred_element_type=F32))


def _outproj(h, f_out, a_out, sg, w_bf):
    rows, d = h.shape
    tm = ROW_TILE
    row_spec = lambda w: pl.BlockSpec((tm, w), lambda i: (i, 0))
    return pl.pallas_call(
        _outproj_kernel,
        grid=(rows // tm,),
        in_specs=[row_spec(d), row_spec(f_out.shape[1]), row_spec(a_out.shape[1]),
                  row_spec(sg.shape[1]), pl.BlockSpec(w_bf.shape, lambda i: (0, 0))],
        out_specs=row_spec(d),
        out_shape=jax.ShapeDtypeStruct((rows, d), F32),
        input_output_aliases={0: 0},
        compiler_params=pltpu.CompilerParams(
            dimension_semantics=("arbitrary",), vmem_limit_bytes=VMEM_LIMIT),
        name="outproj",
    )(h, f_out, a_out, sg, w_bf)


def _orig_positions(seq, l_pad):
    r = jnp.arange(l_pad, dtype=jnp.int32)
    pos = jnp.where(r < seq, r + N_META, r - seq)
    valid = r < seq + N_META
    return jnp.where(valid, pos, 0), valid


def _rope_tables(pos, hd):
    rot = hd // 4
    half = rot // 2
    inv_freq = ROPE_THETA ** (-jnp.arange(0, rot, 2, dtype=F32) / rot)
    ang = pos.astype(F32)[:, None] * inv_freq[None, :]
    cos, sin = jnp.cos(ang), jnp.sin(ang)
    d = jnp.arange(LANES) % hd
    first = d < half
    second = (d >= half) & (d < rot)
    idx = jnp.where(second, d - half, jnp.where(first, d, 0))
    cos_l, sin_l = cos[:, idx], sin[:, idx]
    ct = jnp.where(first | second, cos_l, 1.0)
    sa = jnp.where(first, -sin_l, 0.0)
    sb = jnp.where(second, sin_l, 0.0)
    return ct, sa, sb


def _position_dft(pos, valid, l):
    prod = (pos[:, None] * pos[None, :]) % l
    ang = prod.astype(F32) * (2.0 * math.pi / l)
    ok = valid[:, None] & valid[None, :]
    cm = jnp.where(ok, jnp.cos(ang), 0.0).astype(BF16)
    sm = jnp.where(ok, -jnp.sin(ang), 0.0).astype(BF16)
    return cm, sm


def _channel_dft(n):
    c = jnp.arange(n, dtype=jnp.int32)
    ang = ((c[:, None] * c[None, :]) % n).astype(F32) * (2.0 * math.pi / n)
    return jnp.concatenate([jnp.cos(ang), jnp.sin(ang)], axis=1).astype(BF16)


def kernel(x, meta_tokens, norm_gain, w_in, w_fourier, q_norm_gain, k_norm_gain,
           lambda_q1, lambda_k1, lambda_q2, lambda_k2, subln_gain, w_out):
    b, seq, d = x.shape
    depth = w_in.shape[0]
    hd = q_norm_gain.shape[1]
    vd = subln_gain.shape[1]
    fw = w_fourier.shape[1] * w_fourier.shape[2]
    gd = w_fourier.shape[2]
    qkw = N_ATTN_HEADS * 2 * hd
    vw = N_ATTN_HEADS * vd
    l = seq + N_META
    l_pad = _padded_len(l)
    assert 2 * hd == LANES and vd == LANES and gd == LANES
    assert l_pad % ROW_TILE == 0 and l_pad % DFT_TILE == 0 and l_pad % Q_TILE == 0

    pos, valid = _orig_positions(seq, l_pad)
    ct, sa, sb = _rope_tables(pos, hd)
    cmat, smat = _position_dft(pos, valid, l)
    cdft = _channel_dft(gd)
    dft_scale = 1.0 / math.sqrt(l * gd)
    qscale = (hd ** -0.5) * math.log2(math.e)

    meta = jnp.broadcast_to(meta_tokens[None].astype(x.dtype), (b, N_META, d))
    pad = jnp.zeros((b, l_pad - l, d), x.dtype)
    h = jnp.concatenate([x, meta, pad], axis=1).reshape(b * l_pad, d)

    w_in_bf = w_in.astype(BF16)
    w_out_bf = w_out.astype(BF16)
    w_f_bf = w_fourier.astype(BF16)

    for li in range(depth):
        lambda_init = 0.8 - 0.6 * math.exp(-0.3 * li)
        qk_gain = jnp.concatenate([jnp.tile(q_norm_gain[li], qkw // hd),
                                   jnp.tile(k_norm_gain[li], qkw // hd)])[None]
        pc, ps, qk, v, sg = _inproj(h, norm_gain[li][None], w_in_bf[li], qk_gain, ct, sa, sb,
                                    cdft, l_pad=l_pad, fw=fw, qkw=qkw, vw=vw, hd=hd,
                                    qscale=qscale)
        f_out = _fourier(cmat, smat, pc.reshape(b, l_pad, fw), ps.reshape(b, l_pad, fw),
                         w_f_bf[li], scale=dft_scale)
        a_out = _attention(qk.reshape(b, l_pad, 2 * qkw), v.reshape(b, l_pad, vw),
                           lambda_q1[li][None], lambda_k1[li][None],
                           lambda_q2[li][None], lambda_k2[li][None], subln_gain[li][None],
                           hd=hd, n_valid=l, lambda_init=lambda_init)
        h = _outproj(h, f_out.reshape(b * l_pad, fw), a_out.reshape(b * l_pad, vw), sg,
                     w_out_bf[li])

    return h.reshape(b, l_pad, d)[:, :seq]
```

```python
import functools
import math

import jax
import jax.numpy as jnp
from jax import lax
from jax.experimental import pallas as pl
from jax.experimental.pallas import tpu as pltpu

F32 = jnp.float32
BF16 = jnp.bfloat16

N_META = 16
N_FOURIER_GROUPS = 4
N_ATTN_HEADS = 4
ROPE_THETA = 500000.0
NORM_EPS = 1e-6

MXU_TILE = 256
LANES = 128
VMEM_LIMIT = 56 * 1024 * 1024

ROW_TILE = 544
DFT_TILE = 544
Q_TILE = 256


def _padded_len(l):
    return -(-l // MXU_TILE) * MXU_TILE


def _inproj_kernel(h_ref, g_ref, w_ref, qkg_ref, ct_ref, sa_ref, sb_ref, cdft_ref,
                   pc_ref, ps_ref, qk_ref, v_ref, sg_ref, *, fw, qkw, vw, hd, qscale):
    x = h_ref[...]
    ms = jnp.mean(x * x, axis=-1, keepdims=True)
    hn = ((x * lax.rsqrt(ms + NORM_EPS)) * g_ref[...]).astype(BF16)

    f = jnp.dot(hn, w_ref[:, 0:fw], preferred_element_type=F32)
    gd = fw // N_FOURIER_GROUPS
    for g in range(N_FOURIER_GROUPS):
        pg = jnp.dot(f[:, g * gd:(g + 1) * gd].astype(BF16), cdft_ref[...],
                     preferred_element_type=F32)
        pc_ref[:, g * gd:(g + 1) * gd] = pg[:, :gd].astype(BF16)
        ps_ref[:, g * gd:(g + 1) * gd] = pg[:, gd:].astype(BF16)

    qk = jnp.dot(hn, w_ref[:, fw:fw + 2 * qkw], preferred_element_type=F32)
    tm = qk.shape[0]
    lo_mask = lax.broadcasted_iota(jnp.int32, (tm, LANES), 1) < hd
    ct = ct_ref[...]
    sa = sa_ref[...]
    sb = sb_ref[...]
    for c in range(2 * qkw // LANES):
        xc = qk[:, c * LANES:(c + 1) * LANES]
        x2 = xc * xc
        lo = jnp.sum(jnp.where(lo_mask, x2, 0.0), axis=-1, keepdims=True)
        hi = jnp.sum(jnp.where(lo_mask, 0.0, x2), axis=-1, keepdims=True)
        msc = jnp.where(lo_mask, lo, hi) * (1.0 / hd)
        y = (xc * lax.rsqrt(msc + NORM_EPS)) * qkg_ref[:, c * LANES:(c + 1) * LANES]
        yr = (y * ct + pltpu.roll(y, LANES - 8, 1) * sa + pltpu.roll(y, 8, 1) * sb)
        if c < qkw // LANES:
            yr = yr * qscale
        qk_ref[:, c * LANES:(c + 1) * LANES] = yr.astype(BF16)

    v = jnp.dot(hn, w_ref[:, fw + 2 * qkw:fw + 2 * qkw + vw], preferred_element_type=F32)
    v_ref[...] = v.astype(BF16)

    gate = jnp.dot(hn, w_ref[:, fw + 2 * qkw + vw:], preferred_element_type=F32)
    sg_ref[...] = gate * (1.0 / (1.0 + jnp.exp(-gate)))


def _inproj(h, gain, w_bf, qk_gain, ct, sa, sb, cdft, *, l_pad, fw, qkw, vw, hd, qscale):
    rows, d = h.shape
    tm = ROW_TILE
    n_pos_tiles = l_pad // tm
    mixw = w_bf.shape[1] - (fw + 2 * qkw + vw)
    row_spec = lambda w: pl.BlockSpec((tm, w), lambda i: (i, 0))
    const = lambda shape: pl.BlockSpec(shape, lambda i: (0,) * len(shape))
    pos_spec = pl.BlockSpec((tm, LANES), lambda i: (i % n_pos_tiles, 0))
    return pl.pallas_call(
        functools.partial(_inproj_kernel, fw=fw, qkw=qkw, vw=vw, hd=hd, qscale=qscale),
        grid=(rows // tm,),
        in_specs=[row_spec(d), const((1, d)), const(w_bf.shape), const((1, 2 * qkw)),
                  pos_spec, pos_spec, pos_spec, const(cdft.shape)],
        out_specs=[row_spec(fw), row_spec(fw), row_spec(2 * qkw), row_spec(vw), row_spec(mixw)],
        out_shape=[jax.ShapeDtypeStruct((rows, fw), BF16),
                   jax.ShapeDtypeStruct((rows, fw), BF16),
                   jax.ShapeDtypeStruct((rows, 2 * qkw), BF16),
                   jax.ShapeDtypeStruct((rows, vw), BF16),
                   jax.ShapeDtypeStruct((rows, mixw), F32)],
        compiler_params=pltpu.CompilerParams(
            dimension_semantics=("arbitrary",), vmem_limit_bytes=VMEM_LIMIT),
        name="inproj",
    )(h, gain, w_bf, qk_gain, ct, sa, sb, cdft)


def _fourier_kernel(cm_ref, sm_ref, pc_ref, ps_ref, wf_ref, o_ref, *, scale):
    re = (jnp.dot(cm_ref[...], pc_ref[0], preferred_element_type=F32)
          + jnp.dot(sm_ref[...], ps_ref[0], preferred_element_type=F32)) * scale
    gd = re.shape[1] // N_FOURIER_GROUPS
    for g in range(N_FOURIER_GROUPS):
        o_ref[0, :, g * gd:(g + 1) * gd] = jnp.dot(
            re[:, g * gd:(g + 1) * gd].astype(BF16), wf_ref[g], preferred_element_type=F32)


def _fourier(cmat, smat, pc, ps, wf_bf, *, scale):
    b, l_pad, fw = pc.shape
    tk = DFT_TILE
    return pl.pallas_call(
        functools.partial(_fourier_kernel, scale=scale),
        grid=(l_pad // tk, b),
        in_specs=[pl.BlockSpec((tk, l_pad), lambda i, j: (i, 0)),
                  pl.BlockSpec((tk, l_pad), lambda i, j: (i, 0)),
                  pl.BlockSpec((1, l_pad, fw), lambda i, j: (j, 0, 0)),
                  pl.BlockSpec((1, l_pad, fw), lambda i, j: (j, 0, 0)),
                  pl.BlockSpec(wf_bf.shape, lambda i, j: (0, 0, 0))],
        out_specs=pl.BlockSpec((1, tk, fw), lambda i, j: (j, i, 0)),
        out_shape=jax.ShapeDtypeStruct((b, l_pad, fw), F32),
        compiler_params=pltpu.CompilerParams(
            dimension_semantics=("arbitrary", "arbitrary"), vmem_limit_bytes=VMEM_LIMIT),
        name="fourier",
    )(cmat, smat, pc, ps, wf_bf)


def _attn_kernel(q_ref, k_ref, v_ref, lq1_ref, lk1_ref, lq2_ref, lk2_ref, sub_ref, o_ref,
                 *, hd, n_valid, lambda_init):
    q = q_ref[0]
    k = k_ref[0]
    v = v_ref[0]
    l_pad = k.shape[0]
    tq = q.shape[0]
    n_main = (n_valid // MXU_TILE) * MXU_TILE
    lane = lax.broadcasted_iota(jnp.int32, q.shape, 1)
    tail_valid = (lax.broadcasted_iota(jnp.int32, (l_pad - n_main, 2 * tq), 0) + n_main) < n_valid
    zero = jnp.zeros_like(q)
    contract_last = (((1,), (1,)), ((), ()))
    contract_first = (((0,), (0,)), ((), ()))

    q2 = jnp.concatenate([jnp.where(lane < hd, q, zero), jnp.where(lane < hd, zero, q)], axis=0)
    s = lax.dot_general(k, q2, contract_last, preferred_element_type=F32)
    s_main = s[:n_main]
    s_tail = jnp.where(tail_valid, s[n_main:], -jnp.inf)
    m = jnp.maximum(jnp.max(s_main, axis=0, keepdims=True),
                    jnp.max(s_tail, axis=0, keepdims=True))
    p_main = jnp.exp2(s_main - m)
    p_tail = jnp.exp2(s_tail - m)
    l = jnp.sum(p_main, axis=0, keepdims=True) + jnp.sum(p_tail, axis=0, keepdims=True)
    o_t = (lax.dot_general(v[:n_main], p_main.astype(BF16), contract_first,
                           preferred_element_type=F32)
           + lax.dot_general(v[n_main:], p_tail.astype(BF16), contract_first,
                             preferred_element_type=F32)) * (1.0 / l)
    outs = [o_t[:, :tq], o_t[:, tq:]]

    lam = (jnp.exp(jnp.sum(lq1_ref[...] * lk1_ref[...], axis=-1, keepdims=True))
           - jnp.exp(jnp.sum(lq2_ref[...] * lk2_ref[...], axis=-1, keepdims=True))
           + lambda_init)
    o = (outs[0] - lam * outs[1]).T
    ms = jnp.mean(o * o, axis=-1, keepdims=True)
    o_ref[0] = ((o * lax.rsqrt(ms + NORM_EPS)) * sub_ref[...]) * (1.0 - lambda_init)


def _attention(qk, v, lq1, lk1, lq2, lk2, sub_gain, *, hd, n_valid, lambda_init):
    b, l_pad, _ = qk.shape
    vd = 2 * hd
    tq = Q_TILE
    vec = lambda n: pl.BlockSpec((1, n), lambda bi, h, qi: (0, 0))
    return pl.pallas_call(
        functools.partial(_attn_kernel, hd=hd, n_valid=n_valid, lambda_init=lambda_init),
        grid=(b, N_ATTN_HEADS, l_pad // tq),
        in_specs=[pl.BlockSpec((1, tq, 2 * hd), lambda bi, h, qi: (bi, qi, h)),
                  pl.BlockSpec((1, l_pad, 2 * hd), lambda bi, h, qi: (bi, 0, N_ATTN_HEADS + h)),
                  pl.BlockSpec((1, l_pad, vd), lambda bi, h, qi: (bi, 0, h)),
                  vec(hd), vec(hd), vec(hd), vec(hd), vec(vd)],
        out_specs=pl.BlockSpec((1, tq, vd), lambda bi, h, qi: (bi, qi, h)),
        out_shape=jax.ShapeDtypeStruct((b, l_pad, N_ATTN_HEADS * vd), F32),
        compiler_params=pltpu.CompilerParams(
            dimension_semantics=("arbitrary", "arbitrary", "arbitrary"),
            vmem_limit_bytes=VMEM_LIMIT),
        name="diffattn",
    )(qk, qk, v, lq1, lk1, lq2, lk2, sub_gain)


def _outproj_kernel(h_ref, f_ref, a_ref, sg_ref, w_ref, o_ref):
    fw = f_ref.shape[1]
    sg = sg_ref[...]
    yf = (f_ref[...] * sg[:, :fw]).astype(BF16)
    ya = (a_ref[...] * sg[:, fw:]).astype(BF16)
    o_ref[...] = (h_ref[...]
                  + jnp.dot(yf, w_ref[:fw, :], preferred_element_type=F32)
                  + jnp.dot(ya, w_ref[fw:, :], preferred_element_type=F32))


def _outproj(h, f_out, a_out, sg, w_bf):
    rows, d = h.shape
    tm = ROW_TILE
    row_spec = lambda w: pl.BlockSpec((tm, w), lambda i: (i, 0))
    return pl.pallas_call(
        _outproj_kernel,
        grid=(rows // tm,),
        in_specs=[row_spec(d), row_spec(f_out.shape[1]), row_spec(a_out.shape[1]),
                  row_spec(sg.shape[1]), pl.BlockSpec(w_bf.shape, lambda i: (0, 0))],
        out_specs=row_spec(d),
        out_shape=jax.ShapeDtypeStruct((rows, d), F32),
        input_output_aliases={0: 0},
        compiler_params=pltpu.CompilerParams(
            dimension_semantics=("arbitrary",), vmem_limit_bytes=VMEM_LIMIT),
        name="outproj",
    )(h, f_out, a_out, sg, w_bf)


def _orig_positions(seq, l_pad):
    r = jnp.arange(l_pad, dtype=jnp.int32)
    pos = jnp.where(r < seq, r + N_META, r - seq)
    valid = r < seq + N_META
    return jnp.where(valid, pos, 0), valid


def _rope_tables(pos, hd):
    rot = hd // 4
    half = rot // 2
    inv_freq = ROPE_THETA ** (-jnp.arange(0, rot, 2, dtype=F32) / rot)
    ang = pos.astype(F32)[:, None] * inv_freq[None, :]
    cos, sin = jnp.cos(ang), jnp.sin(ang)
    d = jnp.arange(LANES) % hd
    first = d < half
    second = (d >= half) & (d < rot)
    idx = jnp.where(second, d - half, jnp.where(first, d, 0))
    cos_l, sin_l = cos[:, idx], sin[:, idx]
    ct = jnp.where(first | second, cos_l, 1.0)
    sa = jnp.where(first, -sin_l, 0.0)
    sb = jnp.where(second, sin_l, 0.0)
    return ct, sa, sb


def _position_dft(pos, valid, l):
    l_pad = pos.shape[0]
    blk = 64
    def trig(k):
        ang = ((k[:, None] * pos[None, :]) % l).astype(F32) * (2.0 * math.pi / l)
        return jnp.cos(ang), jnp.sin(ang)
    ch, sh = trig(jnp.arange(l_pad // blk, dtype=jnp.int32) * blk)
    cl, sl = trig(jnp.arange(blk, dtype=jnp.int32) + N_META)
    ok = (valid[:, None] & valid[None, :]).reshape(l_pad // blk, blk, l_pad)
    cm = ch[:, None, :] * cl[None, :, :] - sh[:, None, :] * sl[None, :, :]
    sm = sh[:, None, :] * cl[None, :, :] + ch[:, None, :] * sl[None, :, :]
    cm = jnp.where(ok, cm, 0.0).astype(BF16).reshape(l_pad, l_pad)
    sm = jnp.where(ok, -sm, 0.0).astype(BF16).reshape(l_pad, l_pad)
    return cm, sm


def _channel_dft(n):
    c = jnp.arange(n, dtype=jnp.int32)
    ang = ((c[:, None] * c[None, :]) % n).astype(F32) * (2.0 * math.pi / n)
    return jnp.concatenate([jnp.cos(ang), jnp.sin(ang)], axis=1).astype(BF16)


def kernel(x, meta_tokens, norm_gain, w_in, w_fourier, q_norm_gain, k_norm_gain,
           lambda_q1, lambda_k1, lambda_q2, lambda_k2, subln_gain, w_out):
    b, seq, d = x.shape
    depth = w_in.shape[0]
    hd = q_norm_gain.shape[1]
    vd = subln_gain.shape[1]
    fw = w_fourier.shape[1] * w_fourier.shape[2]
    gd = w_fourier.shape[2]
    qkw = N_ATTN_HEADS * 2 * hd
    vw = N_ATTN_HEADS * vd
    l = seq + N_META
    l_pad = _padded_len(l)
    assert 2 * hd == LANES and vd == LANES and gd == LANES
    assert l_pad % ROW_TILE == 0 and l_pad % DFT_TILE == 0 and l_pad % Q_TILE == 0

    pos, valid = _orig_positions(seq, l_pad)
    ct, sa, sb = _rope_tables(pos, hd)
    cmat, smat = _position_dft(pos, valid, l)
    cdft = _channel_dft(gd)
    dft_scale = 1.0 / math.sqrt(l * gd)
    qscale = (hd ** -0.5) * math.log2(math.e)

    meta = jnp.broadcast_to(meta_tokens[None].astype(x.dtype), (b, N_META, d))
    pad = jnp.zeros((b, l_pad - l, d), x.dtype)
    h = jnp.concatenate([x, meta, pad], axis=1).reshape(b * l_pad, d)

    w_in_bf = w_in.astype(BF16)
    w_out_bf = w_out.astype(BF16)
    w_f_bf = w_fourier.astype(BF16)

    for li in range(depth):
        lambda_init = 0.8 - 0.6 * math.exp(-0.3 * li)
        qk_gain = jnp.concatenate([jnp.tile(q_norm_gain[li], qkw // hd),
                                   jnp.tile(k_norm_gain[li], qkw // hd)])[None]
        pc, ps, qk, v, sg = _inproj(h, norm_gain[li][None], w_in_bf[li], qk_gain, ct, sa, sb,
                                    cdft, l_pad=l_pad, fw=fw, qkw=qkw, vw=vw, hd=hd,
                                    qscale=qscale)
        f_out = _fourier(cmat, smat, pc.reshape(b, l_pad, fw), ps.reshape(b, l_pad, fw),
                         w_f_bf[li], scale=dft_scale)
        a_out = _attention(qk.reshape(b, l_pad, 2 * qkw), v.reshape(b, l_pad, vw),
                           lambda_q1[li][None], lambda_k1[li][None],
                           lambda_q2[li][None], lambda_k2[li][None], subln_gain[li][None],
                           hd=hd, n_valid=l, lambda_init=lambda_init)
        h = _outproj(h, f_out.reshape(b * l_pad, fw), a_out.reshape(b * l_pad, vw), sg,
                     w_out_bf[li])

    return h.reshape(b, l_pad, d)[:, :seq]
```

```python
import functools
import math

import jax
import jax.numpy as jnp
from jax import lax
from jax.experimental import pallas as pl
from jax.experimental.pallas import tpu as pltpu

F32 = jnp.float32
BF16 = jnp.bfloat16

N_META = 16
N_FOURIER_GROUPS = 4
N_ATTN_HEADS = 4
ROPE_THETA = 500000.0
NORM_EPS = 1e-6

MXU_TILE = 256
LANES = 128
VMEM_LIMIT = 56 * 1024 * 1024

ROW_TILE = 544
DFT_TILE = 544
Q_TILE = 256
MAX_UNSHIFTED_SCORE = 64.0


def _half_len(l):
    return -(-(l // 2 + 1) // LANES) * LANES


def _inproj_kernel(h_ref, g_ref, w_ref, qkg_ref, ct_ref, sa_ref, sb_ref, cdft_ref,
                   pc_ref, ps_ref, qk_ref, v_ref, sg_ref, *, fw, qkw, vw, hd, qscale):
    x = h_ref[...]
    ms = jnp.mean(x * x, axis=-1, keepdims=True)
    hn = ((x * lax.rsqrt(ms + NORM_EPS)) * g_ref[...]).astype(BF16)

    f = jnp.dot(hn, w_ref[:, 0:fw], preferred_element_type=F32)
    gd = fw // N_FOURIER_GROUPS
    for g in range(N_FOURIER_GROUPS):
        pg = jnp.dot(f[:, g * gd:(g + 1) * gd].astype(BF16), cdft_ref[...],
                     preferred_element_type=F32)
        pc_ref[:, g * gd:(g + 1) * gd] = pg[:, :gd].astype(BF16)
        ps_ref[:, g * gd:(g + 1) * gd] = pg[:, gd:].astype(BF16)

    qk = jnp.dot(hn, w_ref[:, fw:fw + 2 * qkw], preferred_element_type=F32)
    tm = qk.shape[0]
    lo_mask = lax.broadcasted_iota(jnp.int32, (tm, LANES), 1) < hd
    ct = ct_ref[...]
    sa = sa_ref[...]
    sb = sb_ref[...]
    for c in range(2 * qkw // LANES):
        xc = qk[:, c * LANES:(c + 1) * LANES]
        x2 = xc * xc
        lo = jnp.sum(jnp.where(lo_mask, x2, 0.0), axis=-1, keepdims=True)
        hi = jnp.sum(jnp.where(lo_mask, 0.0, x2), axis=-1, keepdims=True)
        msc = jnp.where(lo_mask, lo, hi) * (1.0 / hd)
        y = (xc * lax.rsqrt(msc + NORM_EPS)) * qkg_ref[:, c * LANES:(c + 1) * LANES]
        yr = (y * ct + pltpu.roll(y, LANES - 8, 1) * sa + pltpu.roll(y, 8, 1) * sb)
        if c < qkw // LANES:
            yr = yr * qscale
        qk_ref[:, c * LANES:(c + 1) * LANES] = yr.astype(BF16)

    v = jnp.dot(hn, w_ref[:, fw + 2 * qkw:fw + 2 * qkw + vw], preferred_element_type=F32)
    v_ref[...] = v.astype(BF16)

    gate = jnp.dot(hn, w_ref[:, fw + 2 * qkw + vw:], preferred_element_type=F32)
    sg_ref[...] = gate * (1.0 / (1.0 + jnp.exp(-gate)))


def _inproj(h, gain, w_bf, qk_gain, ct, sa, sb, cdft, *, l_pad, fw, qkw, vw, hd, qscale):
    rows, d = h.shape
    tm = ROW_TILE
    n_pos_tiles = l_pad // tm
    mixw = w_bf.shape[1] - (fw + 2 * qkw + vw)
    row_spec = lambda w: pl.BlockSpec((tm, w), lambda i: (i, 0))
    const = lambda shape: pl.BlockSpec(shape, lambda i: (0,) * len(shape))
    pos_spec = pl.BlockSpec((tm, LANES), lambda i: (i % n_pos_tiles, 0))
    return pl.pallas_call(
        functools.partial(_inproj_kernel, fw=fw, qkw=qkw, vw=vw, hd=hd, qscale=qscale),
        grid=(rows // tm,),
        in_specs=[row_spec(d), const((1, d)), const(w_bf.shape), const((1, 2 * qkw)),
                  pos_spec, pos_spec, pos_spec, const(cdft.shape)],
        out_specs=[row_spec(fw), row_spec(fw), row_spec(2 * qkw), row_spec(vw), row_spec(mixw)],
        out_shape=[jax.ShapeDtypeStruct((rows, fw), BF16),
                   jax.ShapeDtypeStruct((rows, fw), BF16),
                   jax.ShapeDtypeStruct((rows, 2 * qkw), BF16),
                   jax.ShapeDtypeStruct((rows, vw), BF16),
                   jax.ShapeDtypeStruct((rows, mixw), F32)],
        compiler_params=pltpu.CompilerParams(
            dimension_semantics=("arbitrary",), vmem_limit_bytes=VMEM_LIMIT),
        name="inproj",
    )(h, gain, w_bf, qk_gain, ct, sa, sb, cdft)


def _fourier_kernel(cm_ref, sm_ref, pc_ref, ps_ref, wf_ref, o_ref, pcf_ref, psf_ref, *, scale):
    @pl.when(pl.program_id(1) == 0)
    def _fold():
        pcf_ref[...] = (pc_ref[0, 0].astype(F32) + pc_ref[0, 1].astype(F32)).astype(BF16)
        psf_ref[...] = (ps_ref[0, 0].astype(F32) - ps_ref[0, 1].astype(F32)).astype(BF16)

    even = jnp.dot(cm_ref[...], pcf_ref[...], preferred_element_type=F32) * scale
    odd = jnp.dot(sm_ref[...], psf_ref[...], preferred_element_type=F32) * scale
    gd = even.shape[1] // N_FOURIER_GROUPS
    for half, re in enumerate((even + odd, even - odd)):
        for g in range(N_FOURIER_GROUPS):
            o_ref[0, half, :, g * gd:(g + 1) * gd] = jnp.dot(
                re[:, g * gd:(g + 1) * gd].astype(BF16), wf_ref[g], preferred_element_type=F32)


def _fourier(cmat, smat, pc, ps, wf_bf, *, scale):
    b, _, half, fw = pc.shape
    tk = DFT_TILE
    return pl.pallas_call(
        functools.partial(_fourier_kernel, scale=scale),
        grid=(b, half // tk),
        in_specs=[pl.BlockSpec((tk, half), lambda j, i: (i, 0)),
                  pl.BlockSpec((tk, half), lambda j, i: (i, 0)),
                  pl.BlockSpec((1, 2, half, fw), lambda j, i: (j, 0, 0, 0)),
                  pl.BlockSpec((1, 2, half, fw), lambda j, i: (j, 0, 0, 0)),
                  pl.BlockSpec(wf_bf.shape, lambda j, i: (0, 0, 0))],
        out_specs=pl.BlockSpec((1, 2, tk, fw), lambda j, i: (j, 0, i, 0)),
        out_shape=jax.ShapeDtypeStruct((b, 2, half, fw), F32),
        scratch_shapes=[pltpu.VMEM((half, fw), BF16), pltpu.VMEM((half, fw), BF16)],
        compiler_params=pltpu.CompilerParams(
            dimension_semantics=("arbitrary", "arbitrary"), vmem_limit_bytes=VMEM_LIMIT),
        name="fourier",
    )(cmat, smat, pc, ps, wf_bf)


def _key_segments(l_pad, valid_ranges):
    def clean(lo, hi):
        return any(a <= lo and hi <= b for a, b in valid_ranges)
    segs = []
    for lo in range(0, l_pad, MXU_TILE):
        hi = lo + MXU_TILE
        if segs and segs[-1][2] and clean(lo, hi):
            segs[-1] = (segs[-1][0], hi, True)
        else:
            segs.append((lo, hi, clean(lo, hi)))
    return segs


def _padding_ranges(lo, hi, valid_ranges):
    out, cur = [], lo
    for a, b in sorted(valid_ranges):
        a, b = max(a, lo), min(b, hi)
        if a >= b:
            continue
        if a > cur:
            out.append((cur, a))
        cur = max(cur, b)
    if cur < hi:
        out.append((cur, hi))
    return out


def _attn_kernel(q_ref, k_ref, v_ref, lq1_ref, lk1_ref, lq2_ref, lk2_ref, sub_ref, o_ref,
                 *, hd, valid_ranges, lambda_init, subtract_max):
    q = q_ref[0]
    k = k_ref[0]
    v = v_ref[0]
    l_pad = k.shape[0]
    tq = q.shape[0]
    lane = lax.broadcasted_iota(jnp.int32, q.shape, 1)
    zero = jnp.zeros_like(q)
    contract_last = (((1,), (1,)), ((), ()))
    contract_first = (((0,), (0,)), ((), ()))

    q2 = jnp.concatenate([jnp.where(lane < hd, q, zero), jnp.where(lane < hd, zero, q)], axis=0)
    s = lax.dot_general(k, q2, contract_last, preferred_element_type=F32)

    segs = _key_segments(l_pad, valid_ranges)
    parts = []
    for lo, hi, is_clean in segs:
        sp = s[lo:hi]
        if not is_clean:
            row2 = 2 * lax.broadcasted_iota(jnp.int32, (hi - lo, 2 * tq), 0)
            for a, b in _padding_ranges(lo, hi, valid_ranges):
                w1 = b - a - 1
                sp = jnp.where(jnp.abs(row2 + (2 * (lo - a) - w1)) <= w1, -jnp.inf, sp)
        parts.append(sp)
    if subtract_max:
        m = functools.reduce(jnp.maximum, [jnp.max(sp, axis=0, keepdims=True) for sp in parts])
        parts = [sp - m for sp in parts]
    parts = [jnp.exp2(sp) for sp in parts]
    l = functools.reduce(jnp.add, [jnp.sum(p, axis=0, keepdims=True) for p in parts])
    o_t = functools.reduce(jnp.add, [
        lax.dot_general(v[lo:hi], p.astype(BF16), contract_first, preferred_element_type=F32)
        for (lo, hi, _), p in zip(segs, parts)]) * (1.0 / l)

    lam = (jnp.exp(jnp.sum(lq1_ref[...] * lk1_ref[...], axis=-1, keepdims=True))
           - jnp.exp(jnp.sum(lq2_ref[...] * lk2_ref[...], axis=-1, keepdims=True))
           + lambda_init)
    o = (o_t[:, :tq] - lam * o_t[:, tq:]).T
    ms = jnp.mean(o * o, axis=-1, keepdims=True)
    o_ref[0] = ((o * lax.rsqrt(ms + NORM_EPS)) * sub_ref[...]) * (1.0 - lambda_init)


def _attention(qk, v, lq1, lk1, lq2, lk2, sub_gain, *, hd, valid_ranges, lambda_init,
               subtract_max):
    b, l_pad, _ = qk.shape
    vd = 2 * hd
    tq = Q_TILE
    vec = lambda n: pl.BlockSpec((1, n), lambda bi, h, qi: (0, 0))
    return pl.pallas_call(
        functools.partial(_attn_kernel, hd=hd, valid_ranges=valid_ranges,
                          lambda_init=lambda_init, subtract_max=subtract_max),
        grid=(b, N_ATTN_HEADS, l_pad // tq),
        in_specs=[pl.BlockSpec((1, tq, 2 * hd), lambda bi, h, qi: (bi, qi, h)),
                  pl.BlockSpec((1, l_pad, 2 * hd), lambda bi, h, qi: (bi, 0, N_ATTN_HEADS + h)),
                  pl.BlockSpec((1, l_pad, vd), lambda bi, h, qi: (bi, 0, h)),
                  vec(hd), vec(hd), vec(hd), vec(hd), vec(vd)],
        out_specs=pl.BlockSpec((1, tq, vd), lambda bi, h, qi: (bi, qi, h)),
        out_shape=jax.ShapeDtypeStruct((b, l_pad, N_ATTN_HEADS * vd), F32),
        compiler_params=pltpu.CompilerParams(
            dimension_semantics=("arbitrary", "arbitrary", "arbitrary"),
            vmem_limit_bytes=VMEM_LIMIT),
        name="diffattn",
    )(qk, qk, v, lq1, lk1, lq2, lk2, sub_gain)


def _outproj_kernel(h_ref, f_ref, a_ref, sg_ref, w_ref, o_ref):
    fw = f_ref.shape[1]
    sg = sg_ref[...]
    yf = (f_ref[...] * sg[:, :fw]).astype(BF16)
    ya = (a_ref[...] * sg[:, fw:]).astype(BF16)
    o_ref[...] = (h_ref[...]
                  + jnp.dot(yf, w_ref[:fw, :], preferred_element_type=F32)
                  + jnp.dot(ya, w_ref[fw:, :], preferred_element_type=F32))


def _outproj(h, f_out, a_out, sg, w_bf):
    rows, d = h.shape
    tm = ROW_TILE
    row_spec = lambda w: pl.BlockSpec((tm, w), lambda i: (i, 0))
    return pl.pallas_call(
        _outproj_kernel,
        grid=(rows // tm,),
        in_specs=[row_spec(d), row_spec(f_out.shape[1]), row_spec(a_out.shape[1]),
                  row_spec(sg.shape[1]), pl.BlockSpec(w_bf.shape, lambda i: (0, 0))],
        out_specs=row_spec(d),
        out_shape=jax.ShapeDtypeStruct((rows, d), F32),
        input_output_aliases={0: 0},
        compiler_params=pltpu.CompilerParams(
            dimension_semantics=("arbitrary",), vmem_limit_bytes=VMEM_LIMIT),
        name="outproj",
    )(h, f_out, a_out, sg, w_bf)


def _row_positions(l, half):
    r = jnp.arange(2 * half, dtype=jnp.int32)
    j = r % half
    first = r < half
    valid = jnp.where(first, j <= l // 2, (j >= 1) & (j < l - l // 2))
    pos = jnp.where(first, j, l - j)
    return jnp.where(valid, pos, 0)


def _rope_tables(pos, hd):
    rot = hd // 4
    half = rot // 2
    inv_freq = ROPE_THETA ** (-jnp.arange(0, rot, 2, dtype=F32) / rot)
    ang = pos.astype(F32)[:, None] * inv_freq[None, :]
    cos, sin = jnp.cos(ang), jnp.sin(ang)
    d = jnp.arange(LANES) % hd
    first = d < half
    second = (d >= half) & (d < rot)
    idx = jnp.where(second, d - half, jnp.where(first, d, 0))
    cos_l, sin_l = cos[:, idx], sin[:, idx]
    ct = jnp.where(first | second, cos_l, 1.0)
    sa = jnp.where(first, -sin_l, 0.0)
    sb = jnp.where(second, sin_l, 0.0)
    return ct, sa, sb


def _position_dft(l, half):
    blk = 64
    j = jnp.arange(half, dtype=jnp.int32)
    def trig(k):
        ang = ((k[:, None] * j[None, :]) % l).astype(F32) * (2.0 * math.pi / l)
        return jnp.cos(ang), jnp.sin(ang)
    ch, sh = trig(jnp.arange(half // blk, dtype=jnp.int32) * blk)
    cl, sl = trig(jnp.arange(blk, dtype=jnp.int32))
    valid = j <= l // 2
    ok = (valid[:, None] & valid[None, :]).reshape(half // blk, blk, half)
    cm = ch[:, None, :] * cl[None, :, :] - sh[:, None, :] * sl[None, :, :]
    sm = sh[:, None, :] * cl[None, :, :] + ch[:, None, :] * sl[None, :, :]
    cm = jnp.where(ok, cm, 0.0).astype(BF16).reshape(half, half)
    sm = jnp.where(ok, -sm, 0.0).astype(BF16).reshape(half, half)
    return cm, sm


def _channel_dft(n):
    c = jnp.arange(n, dtype=jnp.int32)
    ang = ((c[:, None] * c[None, :]) % n).astype(F32) * (2.0 * math.pi / n)
    return jnp.concatenate([jnp.cos(ang), jnp.sin(ang)], axis=1).astype(BF16)


def kernel(x, meta_tokens, norm_gain, w_in, w_fourier, q_norm_gain, k_norm_gain,
           lambda_q1, lambda_k1, lambda_q2, lambda_k2, subln_gain, w_out):
    b, seq, d = x.shape
    depth = w_in.shape[0]
    hd = q_norm_gain.shape[1]
    vd = subln_gain.shape[1]
    fw = w_fourier.shape[1] * w_fourier.shape[2]
    gd = w_fourier.shape[2]
    qkw = N_ATTN_HEADS * 2 * hd
    vw = N_ATTN_HEADS * vd
    l = seq + N_META
    half = _half_len(l)
    l_pad = 2 * half
    n_first = l // 2 + 1
    n_second = l - n_first
    assert l % 2 == 0 and N_META < n_first
    assert 2 * hd == LANES and vd == LANES and gd == LANES
    assert l_pad % ROW_TILE == 0 and half % DFT_TILE == 0 and l_pad % Q_TILE == 0
    assert half % 64 == 0
    valid_ranges = ((0, n_first), (half + 1, half + 1 + n_second))

    pos = _row_positions(l, half)
    ct, sa, sb = _rope_tables(pos, hd)
    cmat, smat = _position_dft(l, half)
    cdft = _channel_dft(gd)
    dft_scale = 1.0 / math.sqrt(l * gd)
    qscale = (hd ** -0.5) * math.log2(math.e)

    n_real_first = n_first - N_META
    meta = jnp.broadcast_to(meta_tokens[None].astype(x.dtype), (b, N_META, d))
    zeros = lambda n: jnp.zeros((b, n, d), x.dtype)
    h = jnp.concatenate([meta, x[:, :n_real_first], zeros(half - n_first),
                         zeros(1), x[:, n_real_first:][:, ::-1], zeros(half - 1 - n_second)],
                        axis=1).reshape(b * l_pad, d)

    w_in_bf = w_in.astype(BF16)
    w_out_bf = w_out.astype(BF16)
    w_f_bf = w_fourier.astype(BF16)

    for li in range(depth):
        lambda_init = 0.8 - 0.6 * math.exp(-0.3 * li)
        qk_gain = jnp.concatenate([jnp.tile(q_norm_gain[li], qkw // hd),
                                   jnp.tile(k_norm_gain[li], qkw // hd)])[None]
        pc, ps, qk, v, sg = _inproj(h, norm_gain[li][None], w_in_bf[li], qk_gain, ct, sa, sb,
                                    cdft, l_pad=l_pad, fw=fw, qkw=qkw, vw=vw, hd=hd,
                                    qscale=qscale)
        f_out = _fourier(cmat, smat, pc.reshape(b, 2, half, fw), ps.reshape(b, 2, half, fw),
                         w_f_bf[li], scale=dft_scale)
        score_bound = (jnp.max(jnp.abs(q_norm_gain[li])) * jnp.max(jnp.abs(k_norm_gain[li]))
                       * (hd * qscale * 1.02))
        attend = lambda subtract_max: functools.partial(
            _attention, hd=hd, valid_ranges=valid_ranges, lambda_init=lambda_init,
            subtract_max=subtract_max)
        a_out = lax.cond(score_bound <= MAX_UNSHIFTED_SCORE, attend(False), attend(True),
                         qk.reshape(b, l_pad, 2 * qkw), v.reshape(b, l_pad, vw),
                         lambda_q1[li][None], lambda_k1[li][None],
                         lambda_q2[li][None], lambda_k2[li][None], subln_gain[li][None])
        h = _outproj(h, f_out.reshape(b * l_pad, fw), a_out.reshape(b * l_pad, vw), sg,
                     w_out_bf[li])

    h = h.reshape(b, l_pad, d)
    return jnp.concatenate([h[:, N_META:n_first],
                            h[:, half + 1:half + 1 + n_second][:, ::-1]], axis=1)
```

```python
import functools
import math

import jax
import jax.numpy as jnp
from jax import lax
from jax.experimental import pallas as pl
from jax.experimental.pallas import tpu as pltpu

F32 = jnp.float32
BF16 = jnp.bfloat16

N_META = 16
N_FOURIER_GROUPS = 4
N_ATTN_HEADS = 4
ROPE_THETA = 500000.0
NORM_EPS = 1e-6

MXU_TILE = 256
LANES = 128
VMEM_LIMIT = 56 * 1024 * 1024

ROW_TILE = 544
DFT_TILE = 544
Q_TILE = 256
HEADS_PER_STEP = 4
MAX_UNSHIFTED_SCORE = 64.0


def _half_len(l):
    return -(-(l // 2 + 1) // LANES) * LANES


def _inproj_kernel(h_ref, g_ref, w_ref, qkg_ref, ct_ref, sa_ref, sb_ref, cdft_ref,
                   pc_ref, ps_ref, qk_ref, v_ref, sg_ref, *, fw, qkw, vw, hd, qscale):
    x = h_ref[...]
    ms = jnp.mean(x * x, axis=-1, keepdims=True)
    hn = ((x * lax.rsqrt(ms + NORM_EPS)) * g_ref[...]).astype(BF16)

    f = jnp.dot(hn, w_ref[:, 0:fw], preferred_element_type=F32)
    gd = fw // N_FOURIER_GROUPS
    for g in range(N_FOURIER_GROUPS):
        pg = jnp.dot(f[:, g * gd:(g + 1) * gd].astype(BF16), cdft_ref[...],
                     preferred_element_type=F32)
        pc_ref[:, g * gd:(g + 1) * gd] = pg[:, :gd].astype(BF16)
        ps_ref[:, g * gd:(g + 1) * gd] = pg[:, gd:].astype(BF16)

    qk = jnp.dot(hn, w_ref[:, fw:fw + 2 * qkw], preferred_element_type=F32)
    tm = qk.shape[0]
    lo_mask = lax.broadcasted_iota(jnp.int32, (tm, LANES), 1) < hd
    ct = ct_ref[...]
    sa = sa_ref[...]
    sb = sb_ref[...]
    for c in range(2 * qkw // LANES):
        xc = qk[:, c * LANES:(c + 1) * LANES]
        x2 = xc * xc
        lo = jnp.sum(jnp.where(lo_mask, x2, 0.0), axis=-1, keepdims=True)
        hi = jnp.sum(jnp.where(lo_mask, 0.0, x2), axis=-1, keepdims=True)
        msc = jnp.where(lo_mask, lo, hi) * (1.0 / hd)
        y = (xc * lax.rsqrt(msc + NORM_EPS)) * qkg_ref[:, c * LANES:(c + 1) * LANES]
        yr = (y * ct + pltpu.roll(y, LANES - 8, 1) * sa + pltpu.roll(y, 8, 1) * sb)
        if c < qkw // LANES:
            yr = yr * qscale
        qk_ref[:, c * LANES:(c + 1) * LANES] = yr.astype(BF16)

    v = jnp.dot(hn, w_ref[:, fw + 2 * qkw:fw + 2 * qkw + vw], preferred_element_type=F32)
    v_ref[...] = v.astype(BF16)

    gate = jnp.dot(hn, w_ref[:, fw + 2 * qkw + vw:], preferred_element_type=F32)
    sg_ref[...] = gate * (1.0 / (1.0 + jnp.exp(-gate)))


def _inproj(h, gain, w_bf, qk_gain, ct, sa, sb, cdft, *, l_pad, fw, qkw, vw, hd, qscale):
    rows, d = h.shape
    tm = ROW_TILE
    n_pos_tiles = l_pad // tm
    mixw = w_bf.shape[1] - (fw + 2 * qkw + vw)
    row_spec = lambda w: pl.BlockSpec((tm, w), lambda i: (i, 0))
    const = lambda shape: pl.BlockSpec(shape, lambda i: (0,) * len(shape))
    pos_spec = pl.BlockSpec((tm, LANES), lambda i: (i % n_pos_tiles, 0))
    return pl.pallas_call(
        functools.partial(_inproj_kernel, fw=fw, qkw=qkw, vw=vw, hd=hd, qscale=qscale),
        grid=(rows // tm,),
        in_specs=[row_spec(d), const((1, d)), const(w_bf.shape), const((1, 2 * qkw)),
                  pos_spec, pos_spec, pos_spec, const(cdft.shape)],
        out_specs=[row_spec(fw), row_spec(fw), row_spec(2 * qkw), row_spec(vw), row_spec(mixw)],
        out_shape=[jax.ShapeDtypeStruct((rows, fw), BF16),
                   jax.ShapeDtypeStruct((rows, fw), BF16),
                   jax.ShapeDtypeStruct((rows, 2 * qkw), BF16),
                   jax.ShapeDtypeStruct((rows, vw), BF16),
                   jax.ShapeDtypeStruct((rows, mixw), F32)],
        compiler_params=pltpu.CompilerParams(
            dimension_semantics=("arbitrary",), vmem_limit_bytes=VMEM_LIMIT),
        name="inproj",
    )(h, gain, w_bf, qk_gain, ct, sa, sb, cdft)


def _fourier_kernel(cm_ref, sm_ref, pc_ref, ps_ref, wf_ref, o_ref, pcf_ref, psf_ref, *, scale):
    @pl.when(pl.program_id(1) == 0)
    def _fold():
        pcf_ref[...] = (pc_ref[0, 0].astype(F32) + pc_ref[0, 1].astype(F32)).astype(BF16)
        psf_ref[...] = (ps_ref[0, 0].astype(F32) - ps_ref[0, 1].astype(F32)).astype(BF16)

    even = jnp.dot(cm_ref[...], pcf_ref[...], preferred_element_type=F32) * scale
    odd = jnp.dot(sm_ref[...], psf_ref[...], preferred_element_type=F32) * scale
    gd = even.shape[1] // N_FOURIER_GROUPS
    for half, re in enumerate((even + odd, even - odd)):
        for g in range(N_FOURIER_GROUPS):
            o_ref[0, half, :, g * gd:(g + 1) * gd] = jnp.dot(
                re[:, g * gd:(g + 1) * gd].astype(BF16), wf_ref[g], preferred_element_type=F32)


def _fourier(cmat, smat, pc, ps, wf_bf, *, scale):
    b, _, half, fw = pc.shape
    tk = DFT_TILE
    return pl.pallas_call(
        functools.partial(_fourier_kernel, scale=scale),
        grid=(b, half // tk),
        in_specs=[pl.BlockSpec((tk, half), lambda j, i: (i, 0)),
                  pl.BlockSpec((tk, half), lambda j, i: (i, 0)),
                  pl.BlockSpec((1, 2, half, fw), lambda j, i: (j, 0, 0, 0)),
                  pl.BlockSpec((1, 2, half, fw), lambda j, i: (j, 0, 0, 0)),
                  pl.BlockSpec(wf_bf.shape, lambda j, i: (0, 0, 0))],
        out_specs=pl.BlockSpec((1, 2, tk, fw), lambda j, i: (j, 0, i, 0)),
        out_shape=jax.ShapeDtypeStruct((b, 2, half, fw), F32),
        scratch_shapes=[pltpu.VMEM((half, fw), BF16), pltpu.VMEM((half, fw), BF16)],
        compiler_params=pltpu.CompilerParams(
            dimension_semantics=("arbitrary", "arbitrary"), vmem_limit_bytes=VMEM_LIMIT),
        name="fourier",
    )(cmat, smat, pc, ps, wf_bf)


def _key_segments(l_pad, valid_ranges):
    def clean(lo, hi):
        return any(a <= lo and hi <= b for a, b in valid_ranges)
    segs = []
    for lo in range(0, l_pad, MXU_TILE):
        hi = lo + MXU_TILE
        if segs and segs[-1][2] and clean(lo, hi):
            segs[-1] = (segs[-1][0], hi, True)
        else:
            segs.append((lo, hi, clean(lo, hi)))
    return segs


def _padding_ranges(lo, hi, valid_ranges):
    out, cur = [], lo
    for a, b in sorted(valid_ranges):
        a, b = max(a, lo), min(b, hi)
        if a >= b:
            continue
        if a > cur:
            out.append((cur, a))
        cur = max(cur, b)
    if cur < hi:
        out.append((cur, hi))
    return out


def _attend_head(q, k, v, lam, sub_gain, *, hd, valid_ranges, lambda_init, subtract_max):
    l_pad = k.shape[0]
    tq = q.shape[0]
    lane = lax.broadcasted_iota(jnp.int32, q.shape, 1)
    zero = jnp.zeros_like(q)
    contract_last = (((1,), (1,)), ((), ()))
    contract_first = (((0,), (0,)), ((), ()))

    q2 = jnp.concatenate([jnp.where(lane < hd, q, zero), jnp.where(lane < hd, zero, q)], axis=0)
    s = lax.dot_general(k, q2, contract_last, preferred_element_type=F32)

    segs = _key_segments(l_pad, valid_ranges)
    parts = []
    for lo, hi, is_clean in segs:
        sp = s[lo:hi]
        if not is_clean:
            row2 = 2 * lax.broadcasted_iota(jnp.int32, (hi - lo, 2 * tq), 0)
            for a, b in _padding_ranges(lo, hi, valid_ranges):
                w1 = b - a - 1
                sp = jnp.where(jnp.abs(row2 + (2 * (lo - a) - w1)) <= w1, -jnp.inf, sp)
        parts.append(sp)
    if subtract_max:
        m = functools.reduce(jnp.maximum, [jnp.max(sp, axis=0, keepdims=True) for sp in parts])
        parts = [sp - m for sp in parts]
    parts = [jnp.exp2(sp) for sp in parts]
    l = functools.reduce(jnp.add, [jnp.sum(p, axis=0, keepdims=True) for p in parts])
    o_t = functools.reduce(jnp.add, [
        lax.dot_general(v[lo:hi], p.astype(BF16), contract_first, preferred_element_type=F32)
        for (lo, hi, _), p in zip(segs, parts)]) * (1.0 / l)
    o = (o_t[:, :tq] - lam * o_t[:, tq:]).T
    ms = jnp.mean(o * o, axis=-1, keepdims=True)
    return ((o * lax.rsqrt(ms + NORM_EPS)) * sub_gain) * (1.0 - lambda_init)


def _attn_kernel(q_ref, k_ref, v_ref, lq1_ref, lk1_ref, lq2_ref, lk2_ref, sub_ref, o_ref,
                 *, hd, lambda_init, **kw):
    lam = (jnp.exp(jnp.sum(lq1_ref[...] * lk1_ref[...], axis=-1, keepdims=True))
           - jnp.exp(jnp.sum(lq2_ref[...] * lk2_ref[...], axis=-1, keepdims=True))
           + lambda_init)
    w = 2 * hd
    for i in range(q_ref.shape[2] // w):
        cols = slice(i * w, (i + 1) * w)
        o_ref[0, :, cols] = _attend_head(q_ref[0, :, cols], k_ref[0, :, cols], v_ref[0, :, cols],
                                         lam, sub_ref[...], hd=hd, lambda_init=lambda_init, **kw)


def _attention(qk, v, lq1, lk1, lq2, lk2, sub_gain, *, hd, valid_ranges, lambda_init,
               subtract_max):
    b, l_pad, _ = qk.shape
    vd = 2 * hd
    tq = Q_TILE
    hp = HEADS_PER_STEP
    n_groups = N_ATTN_HEADS // hp
    vec = lambda n: pl.BlockSpec((1, n), lambda bi, h, qi: (0, 0))
    return pl.pallas_call(
        functools.partial(_attn_kernel, hd=hd, valid_ranges=valid_ranges,
                          lambda_init=lambda_init, subtract_max=subtract_max),
        grid=(b, n_groups, l_pad // tq),
        in_specs=[pl.BlockSpec((1, tq, hp * 2 * hd), lambda bi, h, qi: (bi, qi, h)),
                  pl.BlockSpec((1, l_pad, hp * 2 * hd), lambda bi, h, qi: (bi, 0, n_groups + h)),
                  pl.BlockSpec((1, l_pad, hp * vd), lambda bi, h, qi: (bi, 0, h)),
                  vec(hd), vec(hd), vec(hd), vec(hd), vec(vd)],
        out_specs=pl.BlockSpec((1, tq, hp * vd), lambda bi, h, qi: (bi, qi, h)),
        out_shape=jax.ShapeDtypeStruct((b, l_pad, N_ATTN_HEADS * vd), F32),
        compiler_params=pltpu.CompilerParams(
            dimension_semantics=("arbitrary", "arbitrary", "arbitrary"),
            vmem_limit_bytes=VMEM_LIMIT),
        name="diffattn",
    )(qk, qk, v, lq1, lk1, lq2, lk2, sub_gain)


def _outproj_kernel(h_ref, f_ref, a_ref, sg_ref, w_ref, o_ref):
    fw = f_ref.shape[1]
    sg = sg_ref[...]
    yf = (f_ref[...] * sg[:, :fw]).astype(BF16)
    ya = (a_ref[...] * sg[:, fw:]).astype(BF16)
    o_ref[...] = (h_ref[...]
                  + jnp.dot(yf, w_ref[:fw, :], preferred_element_type=F32)
                  + jnp.dot(ya, w_ref[fw:, :], preferred_element_type=F32))


def _outproj(h, f_out, a_out, sg, w_bf):
    rows, d = h.shape
    tm = ROW_TILE
    row_spec = lambda w: pl.BlockSpec((tm, w), lambda i: (i, 0))
    return pl.pallas_call(
        _outproj_kernel,
        grid=(rows // tm,),
        in_specs=[row_spec(d), row_spec(f_out.shape[1]), row_spec(a_out.shape[1]),
                  row_spec(sg.shape[1]), pl.BlockSpec(w_bf.shape, lambda i: (0, 0))],
        out_specs=row_spec(d),
        out_shape=jax.ShapeDtypeStruct((rows, d), F32),
        input_output_aliases={0: 0},
        compiler_params=pltpu.CompilerParams(
            dimension_semantics=("arbitrary",), vmem_limit_bytes=VMEM_LIMIT),
        name="outproj",
    )(h, f_out, a_out, sg, w_bf)


def _reverse_kernel(lo_ref, hi_ref, *rest, anti_diag, valid_lo, valid_hi):
    o_ref = rest[-1]
    r = o_ref.shape[1]
    src = jnp.concatenate([lo_ref[0], hi_ref[0]], axis=0)
    ri = lax.broadcasted_iota(jnp.int32, (r, 2 * r), 0)
    ci = lax.broadcasted_iota(jnp.int32, (r, 2 * r), 1)
    perm = jnp.where(ri + ci == anti_diag, 1.0, 0.0).astype(BF16)
    p0 = src.astype(BF16)
    r1 = src - p0.astype(F32)
    p1 = r1.astype(BF16)
    p2 = (r1 - p1.astype(F32)).astype(BF16)
    out = (jnp.dot(perm, p0, preferred_element_type=F32)
           + jnp.dot(perm, p1, preferred_element_type=F32)
           + jnp.dot(perm, p2, preferred_element_type=F32))
    a = pl.program_id(1) * r + lax.broadcasted_iota(jnp.int32, (r, 1), 0)
    inside = jnp.abs(2 * a - (valid_lo + valid_hi - 1)) <= (valid_hi - valid_lo - 1)
    o_ref[0] = jnp.where(inside, out, 0.0)


def _reverse_rows(src, *, src_tile0, total, n_dst_tiles, valid, dst=None, dst_tile0=0):
    b, n_src_rows, d = src.shape
    r = LANES
    q0, rem = divmod(total - (r - 1), r)
    n_src_tiles = n_src_rows // r
    clamp = lambda u: jnp.clip(u + src_tile0, 0, n_src_tiles - 1)
    in_specs = [pl.BlockSpec((1, r, d), lambda bi, t: (bi, clamp(q0 - t), 0)),
                pl.BlockSpec((1, r, d), lambda bi, t: (bi, clamp(q0 - t + 1), 0))]
    args = [src, src]
    aliases = {}
    if dst is None:
        out_shape = jax.ShapeDtypeStruct((b, n_dst_tiles * r, d), src.dtype)
    else:
        out_shape = jax.ShapeDtypeStruct(dst.shape, dst.dtype)
        in_specs.append(pl.BlockSpec(memory_space=pl.ANY))
        args.append(dst)
        aliases = {2: 0}
    return pl.pallas_call(
        functools.partial(_reverse_kernel, anti_diag=r - 1 + rem, valid_lo=valid[0],
                          valid_hi=valid[1]),
        grid=(b, n_dst_tiles),
        in_specs=in_specs,
        out_specs=pl.BlockSpec((1, r, d), lambda bi, t: (bi, dst_tile0 + t, 0)),
        out_shape=out_shape,
        input_output_aliases=aliases,
        compiler_params=pltpu.CompilerParams(
            dimension_semantics=("arbitrary", "arbitrary"), vmem_limit_bytes=VMEM_LIMIT),
        name="reverse_rows",
    )(*args)


def _row_positions(l, half):
    r = jnp.arange(2 * half, dtype=jnp.int32)
    j = r % half
    first = r < half
    valid = jnp.where(first, j <= l // 2, (j >= 1) & (j < l - l // 2))
    pos = jnp.where(first, j, l - j)
    return jnp.where(valid, pos, 0)


def _rope_tables(pos, hd):
    rot = hd // 4
    half = rot // 2
    inv_freq = ROPE_THETA ** (-jnp.arange(0, rot, 2, dtype=F32) / rot)
    ang = pos.astype(F32)[:, None] * inv_freq[None, :]
    cos, sin = jnp.cos(ang), jnp.sin(ang)
    d = jnp.arange(LANES) % hd
    first = d < half
    second = (d >= half) & (d < rot)
    idx = jnp.where(second, d - half, jnp.where(first, d, 0))
    cos_l, sin_l = cos[:, idx], sin[:, idx]
    ct = jnp.where(first | second, cos_l, 1.0)
    sa = jnp.where(first, -sin_l, 0.0)
    sb = jnp.where(second, sin_l, 0.0)
    return ct, sa, sb


def _position_dft(l, half):
    blk = 64
    j = jnp.arange(half, dtype=jnp.int32)
    def trig(k):
        ang = ((k[:, None] * j[None, :]) % l).astype(F32) * (2.0 * math.pi / l)
        return jnp.cos(ang), jnp.sin(ang)
    ch, sh = trig(jnp.arange(half // blk, dtype=jnp.int32) * blk)
    cl, sl = trig(jnp.arange(blk, dtype=jnp.int32))
    valid = j <= l // 2
    ok = (valid[:, None] & valid[None, :]).reshape(half // blk, blk, half)
    cm = ch[:, None, :] * cl[None, :, :] - sh[:, None, :] * sl[None, :, :]
    sm = sh[:, None, :] * cl[None, :, :] + ch[:, None, :] * sl[None, :, :]
    cm = jnp.where(ok, cm, 0.0).astype(BF16).reshape(half, half)
    sm = jnp.where(ok, -sm, 0.0).astype(BF16).reshape(half, half)
    return cm, sm


def _channel_dft(n):
    c = jnp.arange(n, dtype=jnp.int32)
    ang = ((c[:, None] * c[None, :]) % n).astype(F32) * (2.0 * math.pi / n)
    return jnp.concatenate([jnp.cos(ang), jnp.sin(ang)], axis=1).astype(BF16)


def kernel(x, meta_tokens, norm_gain, w_in, w_fourier, q_norm_gain, k_norm_gain,
           lambda_q1, lambda_k1, lambda_q2, lambda_k2, subln_gain, w_out):
    b, seq, d = x.shape
    depth = w_in.shape[0]
    hd = q_norm_gain.shape[1]
    vd = subln_gain.shape[1]
    fw = w_fourier.shape[1] * w_fourier.shape[2]
    gd = w_fourier.shape[2]
    qkw = N_ATTN_HEADS * 2 * hd
    vw = N_ATTN_HEADS * vd
    l = seq + N_META
    half = _half_len(l)
    l_pad = 2 * half
    n_first = l // 2 + 1
    n_second = l - n_first
    assert l % 2 == 0 and N_META < n_first
    assert 2 * hd == LANES and vd == LANES and gd == LANES
    assert l_pad % ROW_TILE == 0 and half % DFT_TILE == 0 and l_pad % Q_TILE == 0
    assert half % 64 == 0
    valid_ranges = ((0, n_first), (half + 1, half + 1 + n_second))

    pos = _row_positions(l, half)
    ct, sa, sb = _rope_tables(pos, hd)
    cmat, smat = _position_dft(l, half)
    cdft = _channel_dft(gd)
    dft_scale = 1.0 / math.sqrt(l * gd)
    qscale = (hd ** -0.5) * math.log2(math.e)

    n_real_first = n_first - N_META
    meta = jnp.broadcast_to(meta_tokens[None].astype(x.dtype), (b, N_META, d))
    h = jnp.concatenate([meta, x[:, :n_real_first], jnp.zeros((b, l_pad - n_first, d), x.dtype)],
                        axis=1)
    h = _reverse_rows(x, src_tile0=0, total=seq, n_dst_tiles=half // LANES,
                      valid=(1, 1 + n_second), dst=h, dst_tile0=half // LANES)
    h = h.reshape(b * l_pad, d)

    w_in_bf = w_in.astype(BF16)
    w_out_bf = w_out.astype(BF16)
    w_f_bf = w_fourier.astype(BF16)

    for li in range(depth):
        lambda_init = 0.8 - 0.6 * math.exp(-0.3 * li)
        qk_gain = jnp.concatenate([jnp.tile(q_norm_gain[li], qkw // hd),
                                   jnp.tile(k_norm_gain[li], qkw // hd)])[None]
        pc, ps, qk, v, sg = _inproj(h, norm_gain[li][None], w_in_bf[li], qk_gain, ct, sa, sb,
                                    cdft, l_pad=l_pad, fw=fw, qkw=qkw, vw=vw, hd=hd,
                                    qscale=qscale)
        f_out = _fourier(cmat, smat, pc.reshape(b, 2, half, fw), ps.reshape(b, 2, half, fw),
                         w_f_bf[li], scale=dft_scale)
        score_bound = (jnp.max(jnp.abs(q_norm_gain[li])) * jnp.max(jnp.abs(k_norm_gain[li]))
                       * (hd * qscale * 1.02))
        attend = lambda subtract_max: functools.partial(
            _attention, hd=hd, valid_ranges=valid_ranges, lambda_init=lambda_init,
            subtract_max=subtract_max)
        a_out = lax.cond(score_bound <= MAX_UNSHIFTED_SCORE, attend(False), attend(True),
                         qk.reshape(b, l_pad, 2 * qkw), v.reshape(b, l_pad, vw),
                         lambda_q1[li][None], lambda_k1[li][None],
                         lambda_q2[li][None], lambda_k2[li][None], subln_gain[li][None])
        h = _outproj(h, f_out.reshape(b * l_pad, fw), a_out.reshape(b * l_pad, vw), sg,
                     w_out_bf[li])

    h = h.reshape(b, l_pad, d)
    tail = _reverse_rows(h, src_tile0=half // LANES, total=n_second, n_dst_tiles=half // LANES,
                         valid=(0, n_second))
    return jnp.concatenate([h[:, N_META:n_first], tail[:, :n_second]], axis=1)
```

```python
import functools
import math

import jax
import jax.numpy as jnp
from jax import lax
from jax.experimental import pallas as pl
from jax.experimental.pallas import tpu as pltpu

F32 = jnp.float32
BF16 = jnp.bfloat16

N_META = 16
N_FOURIER_GROUPS = 4
N_ATTN_HEADS = 4
ROPE_THETA = 500000.0
NORM_EPS = 1e-6

MXU_TILE = 256
LANES = 128
VMEM_LIMIT = 56 * 1024 * 1024

ROW_TILE = 544
DFT_TILE = 544
Q_TILE = 256
HEADS_PER_STEP = 4
MAX_UNSHIFTED_SCORE = 64.0


def _half_len(l):
    return -(-(l // 2 + 1) // LANES) * LANES


def _inproj_kernel(h_ref, g_ref, w_ref, qkg_ref, ct_ref, sa_ref, sb_ref, cdft_ref,
                   pc_ref, ps_ref, qk_ref, v_ref, sg_ref, *, fw, qkw, vw, hd, qscale):
    x = h_ref[...]
    ms = jnp.mean(x * x, axis=-1, keepdims=True)
    hn = ((x * lax.rsqrt(ms + NORM_EPS)) * g_ref[...]).astype(BF16)

    f = jnp.dot(hn, w_ref[:, 0:fw], preferred_element_type=F32)
    gd = fw // N_FOURIER_GROUPS
    for g in range(N_FOURIER_GROUPS):
        pg = jnp.dot(f[:, g * gd:(g + 1) * gd].astype(BF16), cdft_ref[...],
                     preferred_element_type=F32)
        pc_ref[:, g * gd:(g + 1) * gd] = pg[:, :gd].astype(BF16)
        ps_ref[:, g * gd:(g + 1) * gd] = pg[:, gd:].astype(BF16)

    qk = jnp.dot(hn, w_ref[:, fw:fw + 2 * qkw], preferred_element_type=F32)
    tm = qk.shape[0]
    lo_mask = lax.broadcasted_iota(jnp.int32, (tm, LANES), 1) < hd
    ct = ct_ref[...]
    sa = sa_ref[...]
    sb = sb_ref[...]
    for c in range(2 * qkw // LANES):
        xc = qk[:, c * LANES:(c + 1) * LANES]
        x2 = xc * xc
        lo = jnp.sum(jnp.where(lo_mask, x2, 0.0), axis=-1, keepdims=True)
        hi = jnp.sum(jnp.where(lo_mask, 0.0, x2), axis=-1, keepdims=True)
        msc = jnp.where(lo_mask, lo, hi) * (1.0 / hd)
        y = (xc * lax.rsqrt(msc + NORM_EPS)) * qkg_ref[:, c * LANES:(c + 1) * LANES]
        yr = (y * ct + pltpu.roll(y, LANES - 8, 1) * sa + pltpu.roll(y, 8, 1) * sb)
        if c < qkw // LANES:
            yr = yr * qscale
        qk_ref[:, c * LANES:(c + 1) * LANES] = yr.astype(BF16)

    v = jnp.dot(hn, w_ref[:, fw + 2 * qkw:fw + 2 * qkw + vw], preferred_element_type=F32)
    v_ref[...] = v.astype(BF16)

    gate = jnp.dot(hn, w_ref[:, fw + 2 * qkw + vw:], preferred_element_type=F32)
    sg_ref[...] = (gate * (1.0 / (1.0 + jnp.exp(-gate)))).astype(sg_ref.dtype)


def _inproj(h, gain, w_bf, qk_gain, ct, sa, sb, cdft, *, l_pad, fw, qkw, vw, hd, qscale):
    rows, d = h.shape
    tm = ROW_TILE
    n_pos_tiles = l_pad // tm
    mixw = w_bf.shape[1] - (fw + 2 * qkw + vw)
    row_spec = lambda w: pl.BlockSpec((tm, w), lambda i: (i, 0))
    const = lambda shape: pl.BlockSpec(shape, lambda i: (0,) * len(shape))
    pos_spec = pl.BlockSpec((tm, LANES), lambda i: (i % n_pos_tiles, 0))
    return pl.pallas_call(
        functools.partial(_inproj_kernel, fw=fw, qkw=qkw, vw=vw, hd=hd, qscale=qscale),
        grid=(rows // tm,),
        in_specs=[row_spec(d), const((1, d)), const(w_bf.shape), const((1, 2 * qkw)),
                  pos_spec, pos_spec, pos_spec, const(cdft.shape)],
        out_specs=[row_spec(fw), row_spec(fw), row_spec(2 * qkw), row_spec(vw), row_spec(mixw)],
        out_shape=[jax.ShapeDtypeStruct((rows, fw), BF16),
                   jax.ShapeDtypeStruct((rows, fw), BF16),
                   jax.ShapeDtypeStruct((rows, 2 * qkw), BF16),
                   jax.ShapeDtypeStruct((rows, vw), BF16),
                   jax.ShapeDtypeStruct((rows, mixw), BF16)],
        compiler_params=pltpu.CompilerParams(
            dimension_semantics=("arbitrary",), vmem_limit_bytes=VMEM_LIMIT),
        name="inproj",
    )(h, gain, w_bf, qk_gain, ct, sa, sb, cdft)


def _fourier_kernel(cm_ref, sm_ref, pc_ref, ps_ref, wf_ref, o_ref, pcf_ref, psf_ref, *, scale):
    @pl.when(pl.program_id(1) == 0)
    def _fold():
        pcf_ref[...] = (pc_ref[0, 0].astype(F32) + pc_ref[0, 1].astype(F32)).astype(BF16)
        psf_ref[...] = (ps_ref[0, 0].astype(F32) - ps_ref[0, 1].astype(F32)).astype(BF16)

    even = jnp.dot(cm_ref[...], pcf_ref[...], preferred_element_type=F32) * scale
    odd = jnp.dot(sm_ref[...], psf_ref[...], preferred_element_type=F32) * scale
    gd = even.shape[1] // N_FOURIER_GROUPS
    for half, re in enumerate((even + odd, even - odd)):
        for g in range(N_FOURIER_GROUPS):
            o_ref[0, half, :, g * gd:(g + 1) * gd] = jnp.dot(
                re[:, g * gd:(g + 1) * gd].astype(BF16), wf_ref[g],
                preferred_element_type=F32).astype(o_ref.dtype)


def _fourier(cmat, smat, pc, ps, wf_bf, *, scale):
    b, _, half, fw = pc.shape
    tk = DFT_TILE
    return pl.pallas_call(
        functools.partial(_fourier_kernel, scale=scale),
        grid=(b, half // tk),
        in_specs=[pl.BlockSpec((tk, half), lambda j, i: (i, 0)),
                  pl.BlockSpec((tk, half), lambda j, i: (i, 0)),
                  pl.BlockSpec((1, 2, half, fw), lambda j, i: (j, 0, 0, 0)),
                  pl.BlockSpec((1, 2, half, fw), lambda j, i: (j, 0, 0, 0)),
                  pl.BlockSpec(wf_bf.shape, lambda j, i: (0, 0, 0))],
        out_specs=pl.BlockSpec((1, 2, tk, fw), lambda j, i: (j, 0, i, 0)),
        out_shape=jax.ShapeDtypeStruct((b, 2, half, fw), BF16),
        scratch_shapes=[pltpu.VMEM((half, fw), BF16), pltpu.VMEM((half, fw), BF16)],
        compiler_params=pltpu.CompilerParams(
            dimension_semantics=("arbitrary", "arbitrary"), vmem_limit_bytes=VMEM_LIMIT),
        name="fourier",
    )(cmat, smat, pc, ps, wf_bf)


def _key_segments(l_pad, valid_ranges):
    def clean(lo, hi):
        return any(a <= lo and hi <= b for a, b in valid_ranges)
    segs = []
    for lo in range(0, l_pad, MXU_TILE):
        hi = lo + MXU_TILE
        if segs and segs[-1][2] and clean(lo, hi):
            segs[-1] = (segs[-1][0], hi, True)
        else:
            segs.append((lo, hi, clean(lo, hi)))
    return segs


def _padding_ranges(lo, hi, valid_ranges):
    out, cur = [], lo
    for a, b in sorted(valid_ranges):
        a, b = max(a, lo), min(b, hi)
        if a >= b:
            continue
        if a > cur:
            out.append((cur, a))
        cur = max(cur, b)
    if cur < hi:
        out.append((cur, hi))
    return out


def _attend_head(q, k, v, lam, sub_gain, *, hd, valid_ranges, lambda_init, subtract_max):
    l_pad = k.shape[0]
    tq = q.shape[0]
    lane = lax.broadcasted_iota(jnp.int32, q.shape, 1)
    zero = jnp.zeros_like(q)
    contract_last = (((1,), (1,)), ((), ()))
    contract_first = (((0,), (0,)), ((), ()))

    q2 = jnp.concatenate([jnp.where(lane < hd, q, zero), jnp.where(lane < hd, zero, q)], axis=0)
    s = lax.dot_general(k, q2, contract_last, preferred_element_type=F32)

    segs = _key_segments(l_pad, valid_ranges)
    parts = []
    for lo, hi, is_clean in segs:
        sp = s[lo:hi]
        if not is_clean:
            row2 = 2 * lax.broadcasted_iota(jnp.int32, (hi - lo, 2 * tq), 0)
            for a, b in _padding_ranges(lo, hi, valid_ranges):
                w1 = b - a - 1
                sp = jnp.where(jnp.abs(row2 + (2 * (lo - a) - w1)) <= w1, -jnp.inf, sp)
        parts.append(sp)
    if subtract_max:
        m = functools.reduce(jnp.maximum, [jnp.max(sp, axis=0, keepdims=True) for sp in parts])
        parts = [sp - m for sp in parts]
    parts = [jnp.exp2(sp) for sp in parts]
    l = functools.reduce(jnp.add, [jnp.sum(p, axis=0, keepdims=True) for p in parts])
    o_t = functools.reduce(jnp.add, [
        lax.dot_general(v[lo:hi], p.astype(BF16), contract_first, preferred_element_type=F32)
        for (lo, hi, _), p in zip(segs, parts)]) * (1.0 / l)
    o = (o_t[:, :tq] - lam * o_t[:, tq:]).T
    ms = jnp.mean(o * o, axis=-1, keepdims=True)
    return ((o * lax.rsqrt(ms + NORM_EPS)) * sub_gain) * (1.0 - lambda_init)


def _attn_kernel(q_ref, k_ref, v_ref, lq1_ref, lk1_ref, lq2_ref, lk2_ref, sub_ref, o_ref,
                 *, hd, lambda_init, **kw):
    lam = (jnp.exp(jnp.sum(lq1_ref[...] * lk1_ref[...], axis=-1, keepdims=True))
           - jnp.exp(jnp.sum(lq2_ref[...] * lk2_ref[...], axis=-1, keepdims=True))
           + lambda_init)
    w = 2 * hd
    for i in range(q_ref.shape[2] // w):
        cols = slice(i * w, (i + 1) * w)
        o_ref[0, :, cols] = _attend_head(
            q_ref[0, :, cols], k_ref[0, :, cols], v_ref[0, :, cols], lam, sub_ref[...],
            hd=hd, lambda_init=lambda_init, **kw).astype(o_ref.dtype)


def _attention(qk, v, lq1, lk1, lq2, lk2, sub_gain, *, hd, valid_ranges, lambda_init,
               subtract_max):
    b, l_pad, _ = qk.shape
    vd = 2 * hd
    tq = Q_TILE
    hp = HEADS_PER_STEP
    n_groups = N_ATTN_HEADS // hp
    vec = lambda n: pl.BlockSpec((1, n), lambda bi, h, qi: (0, 0))
    return pl.pallas_call(
        functools.partial(_attn_kernel, hd=hd, valid_ranges=valid_ranges,
                          lambda_init=lambda_init, subtract_max=subtract_max),
        grid=(b, n_groups, l_pad // tq),
        in_specs=[pl.BlockSpec((1, tq, hp * 2 * hd), lambda bi, h, qi: (bi, qi, h)),
                  pl.BlockSpec((1, l_pad, hp * 2 * hd), lambda bi, h, qi: (bi, 0, n_groups + h)),
                  pl.BlockSpec((1, l_pad, hp * vd), lambda bi, h, qi: (bi, 0, h)),
                  vec(hd), vec(hd), vec(hd), vec(hd), vec(vd)],
        out_specs=pl.BlockSpec((1, tq, hp * vd), lambda bi, h, qi: (bi, qi, h)),
        out_shape=jax.ShapeDtypeStruct((b, l_pad, N_ATTN_HEADS * vd), BF16),
        compiler_params=pltpu.CompilerParams(
            dimension_semantics=("arbitrary", "arbitrary", "arbitrary"),
            vmem_limit_bytes=VMEM_LIMIT),
        name="diffattn",
    )(qk, qk, v, lq1, lk1, lq2, lk2, sub_gain)


def _outproj_kernel(h_ref, f_ref, a_ref, sg_ref, w_ref, o_ref):
    fw = f_ref.shape[1]
    sg = sg_ref[...]
    yf = f_ref[...] * sg[:, :fw]
    ya = a_ref[...] * sg[:, fw:]
    o_ref[...] = (h_ref[...]
                  + jnp.dot(yf, w_ref[:fw, :], preferred_element_type=F32)
                  + jnp.dot(ya, w_ref[fw:, :], preferred_element_type=F32))


def _outproj(h, f_out, a_out, sg, w_bf):
    rows, d = h.shape
    tm = ROW_TILE
    row_spec = lambda w: pl.BlockSpec((tm, w), lambda i: (i, 0))
    return pl.pallas_call(
        _outproj_kernel,
        grid=(rows // tm,),
        in_specs=[row_spec(d), row_spec(f_out.shape[1]), row_spec(a_out.shape[1]),
                  row_spec(sg.shape[1]), pl.BlockSpec(w_bf.shape, lambda i: (0, 0))],
        out_specs=row_spec(d),
        out_shape=jax.ShapeDtypeStruct((rows, d), F32),
        input_output_aliases={0: 0},
        compiler_params=pltpu.CompilerParams(
            dimension_semantics=("arbitrary",), vmem_limit_bytes=VMEM_LIMIT),
        name="outproj",
    )(h, f_out, a_out, sg, w_bf)


def _assemble_kernel(direct_ref, lo_ref, hi_ref, o_ref, *, anti_diag, direct_end, rev_lo, rev_hi):
    r = o_ref.shape[1]
    t = pl.program_id(1)
    a = t * r + lax.broadcasted_iota(jnp.int32, (r, 1), 0)
    direct = jnp.where(a < direct_end, direct_ref[0], 0.0)

    @pl.when(t < rev_lo // r)
    def _copy():
        o_ref[0] = direct

    @pl.when(t >= rev_lo // r)
    def _reverse():
        src = jnp.concatenate([lo_ref[0], hi_ref[0]], axis=0)
        ri = lax.broadcasted_iota(jnp.int32, (r, 2 * r), 0)
        ci = lax.broadcasted_iota(jnp.int32, (r, 2 * r), 1)
        perm = jnp.where(ri + ci == anti_diag, 1.0, 0.0).astype(BF16)
        p0 = src.astype(BF16)
        r1 = src - p0.astype(F32)
        p1 = r1.astype(BF16)
        p2 = (r1 - p1.astype(F32)).astype(BF16)
        rev = (jnp.dot(perm, p0, preferred_element_type=F32)
               + jnp.dot(perm, p1, preferred_element_type=F32)
               + jnp.dot(perm, p2, preferred_element_type=F32))
        in_rev = jnp.abs(2 * a - (rev_lo + rev_hi - 1)) <= (rev_hi - rev_lo - 1)
        o_ref[0] = jnp.where(in_rev, rev, direct)


def _assemble(direct, src, *, n_rows, direct_end, rev, total, src_tile0, n_src_tiles):
    b, _, d = direct.shape
    r = LANES
    q0, rem = divmod(total - (r - 1), r)
    n_direct_tiles = -(-direct_end // r)
    rev_tile = lambda u: src_tile0 + jnp.clip(u, 0, n_src_tiles - 1)
    return pl.pallas_call(
        functools.partial(_assemble_kernel, anti_diag=r - 1 + rem, direct_end=direct_end,
                          rev_lo=rev[0], rev_hi=rev[1]),
        grid=(b, n_rows // r),
        in_specs=[pl.BlockSpec((1, r, d), lambda bi, t: (bi, jnp.minimum(t, n_direct_tiles - 1), 0)),
                  pl.BlockSpec((1, r, d), lambda bi, t: (bi, rev_tile(q0 - t), 0)),
                  pl.BlockSpec((1, r, d), lambda bi, t: (bi, rev_tile(q0 - t + 1), 0))],
        out_specs=pl.BlockSpec((1, r, d), lambda bi, t: (bi, t, 0)),
        out_shape=jax.ShapeDtypeStruct((b, n_rows, d), direct.dtype),
        compiler_params=pltpu.CompilerParams(
            dimension_semantics=("arbitrary", "arbitrary"), vmem_limit_bytes=VMEM_LIMIT),
        name="assemble",
    )(direct, src, src)


def _row_positions(l, half):
    n_tok = l // 2 + 1 - N_META
    j = jnp.arange(half, dtype=jnp.int32)
    p = jnp.where(j < n_tok, j + N_META, n_tok + N_META - 1 - j)
    valid_a = j < n_tok + N_META
    valid_b = valid_a & (p != 0) & (2 * p != l)
    pos = jnp.concatenate([p, l - p])
    return jnp.where(jnp.concatenate([valid_a, valid_b]), pos, 0)


def _rope_tables(pos, hd):
    rot = hd // 4
    half = rot // 2
    inv_freq = ROPE_THETA ** (-jnp.arange(0, rot, 2, dtype=F32) / rot)
    ang = pos.astype(F32)[:, None] * inv_freq[None, :]
    cos, sin = jnp.cos(ang), jnp.sin(ang)
    d = jnp.arange(LANES) % hd
    first = d < half
    second = (d >= half) & (d < rot)
    idx = jnp.where(second, d - half, jnp.where(first, d, 0))
    cos_l, sin_l = cos[:, idx], sin[:, idx]
    ct = jnp.where(first | second, cos_l, 1.0)
    sa = jnp.where(first, -sin_l, 0.0)
    sb = jnp.where(second, sin_l, 0.0)
    return ct, sa, sb


def _position_dft(pos, l, half):
    blk = 64
    n_tok = l // 2 + 1 - N_META
    p = pos[:half]
    valid = jnp.arange(half) < n_tok + N_META
    def trig(k):
        ang = ((k[:, None] * p[None, :]) % l).astype(F32) * (2.0 * math.pi / l)
        return jnp.cos(ang), jnp.sin(ang)
    ch, sh = trig(jnp.arange(half // blk, dtype=jnp.int32) * blk)
    cl, sl = trig(jnp.arange(blk, dtype=jnp.int32) + N_META)
    cm = (ch[:, None, :] * cl[None, :, :] - sh[:, None, :] * sl[None, :, :]).reshape(half, half)
    sm = (sh[:, None, :] * cl[None, :, :] + ch[:, None, :] * sl[None, :, :]).reshape(half, half)
    cmeta, smeta = trig(p[n_tok:n_tok + N_META])
    cm = lax.dynamic_update_slice(cm, cmeta, (n_tok, 0))
    sm = lax.dynamic_update_slice(sm, smeta, (n_tok, 0))
    ok = valid[:, None] & valid[None, :]
    return jnp.where(ok, cm, 0.0).astype(BF16), jnp.where(ok, -sm, 0.0).astype(BF16)


def _channel_dft(n):
    c = jnp.arange(n, dtype=jnp.int32)
    ang = ((c[:, None] * c[None, :]) % n).astype(F32) * (2.0 * math.pi / n)
    return jnp.concatenate([jnp.cos(ang), jnp.sin(ang)], axis=1).astype(BF16)


def kernel(x, meta_tokens, norm_gain, w_in, w_fourier, q_norm_gain, k_norm_gain,
           lambda_q1, lambda_k1, lambda_q2, lambda_k2, subln_gain, w_out):
    b, seq, d = x.shape
    depth = w_in.shape[0]
    hd = q_norm_gain.shape[1]
    vd = subln_gain.shape[1]
    fw = w_fourier.shape[1] * w_fourier.shape[2]
    gd = w_fourier.shape[2]
    qkw = N_ATTN_HEADS * 2 * hd
    vw = N_ATTN_HEADS * vd
    l = seq + N_META
    half = _half_len(l)
    l_pad = 2 * half
    n_first = l // 2 + 1
    n_tok = n_first - N_META
    n_pair = n_tok - 1
    assert l % 2 == 0 and N_META < n_first and seq % LANES == 0
    assert 2 * hd == LANES and vd == LANES and gd == LANES
    assert l_pad % ROW_TILE == 0 and half % DFT_TILE == 0 and l_pad % Q_TILE == 0
    assert half % 64 == 0
    valid_ranges = ((0, n_first), (half, half + n_pair), (half + n_tok, half + n_first - 1))

    pos = _row_positions(l, half)
    ct, sa, sb = _rope_tables(pos, hd)
    cmat, smat = _position_dft(pos, l, half)
    cdft = _channel_dft(gd)
    dft_scale = 1.0 / math.sqrt(l * gd)
    qscale = (hd ** -0.5) * math.log2(math.e)

    h = _assemble(x, x, n_rows=l_pad, direct_end=n_tok, rev=(half, half + n_pair),
                  total=half + seq - N_META, src_tile0=0, n_src_tiles=seq // LANES)
    meta = jnp.broadcast_to(meta_tokens[::-1][None].astype(x.dtype), (b, N_META, d))
    h = lax.dynamic_update_slice(h, meta, (0, n_tok, 0))
    h = lax.dynamic_update_slice(h, x[:, seq - N_META + 1:], (0, half + n_tok, 0))
    h = h.reshape(b * l_pad, d)

    w_in_bf = w_in.astype(BF16)
    w_out_bf = w_out.astype(BF16)
    w_f_bf = w_fourier.astype(BF16)

    for li in range(depth):
        lambda_init = 0.8 - 0.6 * math.exp(-0.3 * li)
        qk_gain = jnp.concatenate([jnp.tile(q_norm_gain[li], qkw // hd),
                                   jnp.tile(k_norm_gain[li], qkw // hd)])[None]
        pc, ps, qk, v, sg = _inproj(h, norm_gain[li][None], w_in_bf[li], qk_gain, ct, sa, sb,
                                    cdft, l_pad=l_pad, fw=fw, qkw=qkw, vw=vw, hd=hd,
                                    qscale=qscale)
        f_out = _fourier(cmat, smat, pc.reshape(b, 2, half, fw), ps.reshape(b, 2, half, fw),
                         w_f_bf[li], scale=dft_scale)
        score_bound = (jnp.max(jnp.abs(q_norm_gain[li])) * jnp.max(jnp.abs(k_norm_gain[li]))
                       * (hd * qscale * 1.02))
        attend = lambda subtract_max: functools.partial(
            _attention, hd=hd, valid_ranges=valid_ranges, lambda_init=lambda_init,
            subtract_max=subtract_max)
        a_out = lax.cond(score_bound <= MAX_UNSHIFTED_SCORE, attend(False), attend(True),
                         qk.reshape(b, l_pad, 2 * qkw), v.reshape(b, l_pad, vw),
                         lambda_q1[li][None], lambda_k1[li][None],
                         lambda_q2[li][None], lambda_k2[li][None], subln_gain[li][None])
        h = _outproj(h, f_out.reshape(b * l_pad, fw), a_out.reshape(b * l_pad, vw), sg,
                     w_out_bf[li])

    h = h.reshape(b, l_pad, d)
    out = _assemble(h, h, n_rows=seq, direct_end=n_tok, rev=(n_tok, seq - N_META + 1),
                    total=seq - N_META, src_tile0=half // LANES, n_src_tiles=half // LANES)
    return lax.dynamic_update_slice(out, h[:, half + n_tok:half + n_first - 1],
                                    (0, seq - N_META + 1, 0))
```

```python
import functools
import math

import jax
import jax.numpy as jnp
from jax import lax
from jax.experimental import pallas as pl
from jax.experimental.pallas import tpu as pltpu

F32 = jnp.float32
BF16 = jnp.bfloat16

N_META = 16
N_FOURIER_GROUPS = 4
N_ATTN_HEADS = 4
ROPE_THETA = 500000.0
NORM_EPS = 1e-6

MXU_TILE = 256
LANES = 128
VMEM_LIMIT = 56 * 1024 * 1024

ROW_TILE = 544
DFT_TILE = 544
Q_TILE = 256
HEADS_PER_STEP = 4
ASSEMBLE_TILE = 256
MAX_UNSHIFTED_SCORE = 64.0


def _half_len(l):
    return -(-(l // 2 + 1) // LANES) * LANES


def _outproj_update(h_ref, f_ref, a_ref, sg_ref, w_ref):
    fw = f_ref.shape[1]
    sg = sg_ref[...]
    yf = f_ref[...] * sg[:, :fw]
    ya = a_ref[...] * sg[:, fw:]
    return (h_ref[...]
            + jnp.dot(yf, w_ref[:fw, :], preferred_element_type=F32)
            + jnp.dot(ya, w_ref[fw:, :], preferred_element_type=F32))


def _inproj_kernel(h_ref, *refs, **kw):
    _inproj_body(h_ref[...], *refs, **kw)


def _outproj_inproj_kernel(h_ref, f_ref, a_ref, sgp_ref, wo_ref, *refs, **kw):
    hout_ref = refs[7]
    h = _outproj_update(h_ref, f_ref, a_ref, sgp_ref, wo_ref)
    hout_ref[...] = h
    _inproj_body(h, *refs[:7], *refs[8:], **kw)


def _inproj_body(x, g_ref, w_ref, qkg_ref, ct_ref, sa_ref, sb_ref, cdft_ref,
                 pc_ref, ps_ref, qk_ref, v_ref, sg_ref, *, fw, qkw, vw, hd, qscale):
    ms = jnp.mean(x * x, axis=-1, keepdims=True)
    hn = ((x * lax.rsqrt(ms + NORM_EPS)) * g_ref[...]).astype(BF16)

    f = jnp.dot(hn, w_ref[:, 0:fw], preferred_element_type=F32)
    gd = fw // N_FOURIER_GROUPS
    for g in range(N_FOURIER_GROUPS):
        pg = jnp.dot(f[:, g * gd:(g + 1) * gd].astype(BF16), cdft_ref[...],
                     preferred_element_type=F32)
        pc_ref[:, g * gd:(g + 1) * gd] = pg[:, :gd].astype(BF16)
        ps_ref[:, g * gd:(g + 1) * gd] = pg[:, gd:].astype(BF16)

    qk = jnp.dot(hn, w_ref[:, fw:fw + 2 * qkw], preferred_element_type=F32)
    tm = qk.shape[0]
    lo_mask = lax.broadcasted_iota(jnp.int32, (tm, LANES), 1) < hd
    ct = ct_ref[...]
    sa = sa_ref[...]
    sb = sb_ref[...]
    for c in range(2 * qkw // LANES):
        xc = qk[:, c * LANES:(c + 1) * LANES]
        x2 = xc * xc
        lo = jnp.sum(jnp.where(lo_mask, x2, 0.0), axis=-1, keepdims=True)
        hi = jnp.sum(jnp.where(lo_mask, 0.0, x2), axis=-1, keepdims=True)
        msc = jnp.where(lo_mask, lo, hi) * (1.0 / hd)
        y = (xc * lax.rsqrt(msc + NORM_EPS)) * qkg_ref[:, c * LANES:(c + 1) * LANES]
        yr = (y * ct + pltpu.roll(y, LANES - 8, 1) * sa + pltpu.roll(y, 8, 1) * sb)
        if c < qkw // LANES:
            yr = yr * qscale
        qk_ref[:, c * LANES:(c + 1) * LANES] = yr.astype(BF16)

    v = jnp.dot(hn, w_ref[:, fw + 2 * qkw:fw + 2 * qkw + vw], preferred_element_type=F32)
    v_ref[...] = v.astype(BF16)

    gate = jnp.dot(hn, w_ref[:, fw + 2 * qkw + vw:], preferred_element_type=F32)
    sg_ref[...] = (gate * (1.0 / (1.0 + jnp.exp(-gate)))).astype(sg_ref.dtype)


def _inproj(h, gain, w_bf, qk_gain, ct, sa, sb, cdft, *, l_pad, fw, qkw, vw, hd, qscale,
            prev=None):
    rows, d = h.shape
    tm = ROW_TILE
    n_pos_tiles = l_pad // tm
    mixw = w_bf.shape[1] - (fw + 2 * qkw + vw)
    row_spec = lambda w: pl.BlockSpec((tm, w), lambda i: (i, 0))
    const = lambda shape: pl.BlockSpec(shape, lambda i: (0,) * len(shape))
    pos_spec = pl.BlockSpec((tm, LANES), lambda i: (i % n_pos_tiles, 0))
    kw = dict(fw=fw, qkw=qkw, vw=vw, hd=hd, qscale=qscale)
    args = [h, gain, w_bf, qk_gain, ct, sa, sb, cdft]
    in_specs = [row_spec(d), const((1, d)), const(w_bf.shape), const((1, 2 * qkw)),
                pos_spec, pos_spec, pos_spec, const(cdft.shape)]
    out_specs = [row_spec(fw), row_spec(fw), row_spec(2 * qkw), row_spec(vw), row_spec(mixw)]
    out_shape = [jax.ShapeDtypeStruct((rows, fw), BF16),
                 jax.ShapeDtypeStruct((rows, fw), BF16),
                 jax.ShapeDtypeStruct((rows, 2 * qkw), BF16),
                 jax.ShapeDtypeStruct((rows, vw), BF16),
                 jax.ShapeDtypeStruct((rows, mixw), BF16)]
    if prev is None:
        body, name, aliases = functools.partial(_inproj_kernel, **kw), "inproj", {}
    else:
        f_out, a_out, sg_prev, wo_bf = prev
        body, name, aliases = functools.partial(_outproj_inproj_kernel, **kw), "outproj_inproj", {0: 0}
        args[1:1] = [f_out, a_out, sg_prev, wo_bf]
        in_specs[1:1] = [row_spec(f_out.shape[1]), row_spec(a_out.shape[1]),
                         row_spec(sg_prev.shape[1]), const(wo_bf.shape)]
        out_specs.insert(0, row_spec(d))
        out_shape.insert(0, jax.ShapeDtypeStruct((rows, d), F32))
    return pl.pallas_call(
        body,
        grid=(rows // tm,),
        in_specs=in_specs,
        out_specs=out_specs,
        out_shape=out_shape,
        input_output_aliases=aliases,
        compiler_params=pltpu.CompilerParams(
            dimension_semantics=("arbitrary",), vmem_limit_bytes=VMEM_LIMIT),
        name=name,
    )(*args)


def _fourier_kernel(cm_ref, sm_ref, pc_ref, ps_ref, wf_ref, o_ref, pcf_ref, psf_ref, *, scale):
    @pl.when(pl.program_id(1) == 0)
    def _fold():
        pcf_ref[...] = (pc_ref[0, 0].astype(F32) + pc_ref[0, 1].astype(F32)).astype(BF16)
        psf_ref[...] = (ps_ref[0, 0].astype(F32) - ps_ref[0, 1].astype(F32)).astype(BF16)

    even = jnp.dot(cm_ref[...], pcf_ref[...], preferred_element_type=F32) * scale
    odd = jnp.dot(sm_ref[...], psf_ref[...], preferred_element_type=F32) * scale
    gd = even.shape[1] // N_FOURIER_GROUPS
    for half, re in enumerate((even + odd, even - odd)):
        for g in range(N_FOURIER_GROUPS):
            o_ref[0, half, :, g * gd:(g + 1) * gd] = jnp.dot(
                re[:, g * gd:(g + 1) * gd].astype(BF16), wf_ref[g],
                preferred_element_type=F32).astype(o_ref.dtype)


def _fourier(cmat, smat, pc, ps, wf_bf, *, scale):
    b, _, half, fw = pc.shape
    tk = DFT_TILE
    return pl.pallas_call(
        functools.partial(_fourier_kernel, scale=scale),
        grid=(b, half // tk),
        in_specs=[pl.BlockSpec((tk, half), lambda j, i: (i, 0)),
                  pl.BlockSpec((tk, half), lambda j, i: (i, 0)),
                  pl.BlockSpec((1, 2, half, fw), lambda j, i: (j, 0, 0, 0)),
                  pl.BlockSpec((1, 2, half, fw), lambda j, i: (j, 0, 0, 0)),
                  pl.BlockSpec(wf_bf.shape, lambda j, i: (0, 0, 0))],
        out_specs=pl.BlockSpec((1, 2, tk, fw), lambda j, i: (j, 0, i, 0)),
        out_shape=jax.ShapeDtypeStruct((b, 2, half, fw), BF16),
        scratch_shapes=[pltpu.VMEM((half, fw), BF16), pltpu.VMEM((half, fw), BF16)],
        compiler_params=pltpu.CompilerParams(
            dimension_semantics=("arbitrary", "arbitrary"), vmem_limit_bytes=VMEM_LIMIT),
        name="fourier",
    )(cmat, smat, pc, ps, wf_bf)


def _key_segments(l_pad, valid_ranges):
    def clean(lo, hi):
        return any(a <= lo and hi <= b for a, b in valid_ranges)
    segs = []
    for lo in range(0, l_pad, MXU_TILE):
        hi = lo + MXU_TILE
        if segs and segs[-1][2] and clean(lo, hi):
            segs[-1] = (segs[-1][0], hi, True)
        else:
            segs.append((lo, hi, clean(lo, hi)))
    return segs


def _padding_ranges(lo, hi, valid_ranges):
    out, cur = [], lo
    for a, b in sorted(valid_ranges):
        a, b = max(a, lo), min(b, hi)
        if a >= b:
            continue
        if a > cur:
            out.append((cur, a))
        cur = max(cur, b)
    if cur < hi:
        out.append((cur, hi))
    return out


def _attend_head(q, k, v, lam, sub_gain, *, hd, valid_ranges, lambda_init, subtract_max):
    l_pad = k.shape[0]
    tq = q.shape[0]
    lane = lax.broadcasted_iota(jnp.int32, q.shape, 1)
    zero = jnp.zeros_like(q)
    contract_last = (((1,), (1,)), ((), ()))
    contract_first = (((0,), (0,)), ((), ()))

    q2 = jnp.concatenate([jnp.where(lane < hd, q, zero), jnp.where(lane < hd, zero, q)], axis=0)
    s = lax.dot_general(k, q2, contract_last, preferred_element_type=F32)

    segs = _key_segments(l_pad, valid_ranges)
    parts = []
    for lo, hi, is_clean in segs:
        sp = s[lo:hi]
        if not is_clean:
            row2 = 2 * lax.broadcasted_iota(jnp.int32, (hi - lo, 2 * tq), 0)
            for a, b in _padding_ranges(lo, hi, valid_ranges):
                w1 = b - a - 1
                sp = jnp.where(jnp.abs(row2 + (2 * (lo - a) - w1)) <= w1, -jnp.inf, sp)
        parts.append(sp)
    if subtract_max:
        m = functools.reduce(jnp.maximum, [jnp.max(sp, axis=0, keepdims=True) for sp in parts])
        parts = [sp - m for sp in parts]
    parts = [jnp.exp2(sp) for sp in parts]
    l = functools.reduce(jnp.add, [jnp.sum(p, axis=0, keepdims=True) for p in parts])
    o_t = functools.reduce(jnp.add, [
        lax.dot_general(v[lo:hi], p.astype(BF16), contract_first, preferred_element_type=F32)
        for (lo, hi, _), p in zip(segs, parts)]) * (1.0 / l)
    o = (o_t[:, :tq] - lam * o_t[:, tq:]).T
    ms = jnp.mean(o * o, axis=-1, keepdims=True)
    return ((o * lax.rsqrt(ms + NORM_EPS)) * sub_gain) * (1.0 - lambda_init)


def _attn_kernel(q_ref, k_ref, v_ref, lq1_ref, lk1_ref, lq2_ref, lk2_ref, sub_ref, o_ref,
                 *, hd, lambda_init, **kw):
    lam = (jnp.exp(jnp.sum(lq1_ref[...] * lk1_ref[...], axis=-1, keepdims=True))
           - jnp.exp(jnp.sum(lq2_ref[...] * lk2_ref[...], axis=-1, keepdims=True))
           + lambda_init)
    w = 2 * hd
    for i in range(q_ref.shape[2] // w):
        cols = slice(i * w, (i + 1) * w)
        o_ref[0, :, cols] = _attend_head(
            q_ref[0, :, cols], k_ref[0, :, cols], v_ref[0, :, cols], lam, sub_ref[...],
            hd=hd, lambda_init=lambda_init, **kw).astype(o_ref.dtype)


def _attention(qk, v, lq1, lk1, lq2, lk2, sub_gain, *, hd, valid_ranges, lambda_init,
               subtract_max):
    b, l_pad, _ = qk.shape
    vd = 2 * hd
    tq = Q_TILE
    hp = HEADS_PER_STEP
    n_groups = N_ATTN_HEADS // hp
    vec = lambda n: pl.BlockSpec((1, n), lambda bi, h, qi: (0, 0))
    return pl.pallas_call(
        functools.partial(_attn_kernel, hd=hd, valid_ranges=valid_ranges,
                          lambda_init=lambda_init, subtract_max=subtract_max),
        grid=(b, n_groups, l_pad // tq),
        in_specs=[pl.BlockSpec((1, tq, hp * 2 * hd), lambda bi, h, qi: (bi, qi, h)),
                  pl.BlockSpec((1, l_pad, hp * 2 * hd), lambda bi, h, qi: (bi, 0, n_groups + h)),
                  pl.BlockSpec((1, l_pad, hp * vd), lambda bi, h, qi: (bi, 0, h)),
                  vec(hd), vec(hd), vec(hd), vec(hd), vec(vd)],
        out_specs=pl.BlockSpec((1, tq, hp * vd), lambda bi, h, qi: (bi, qi, h)),
        out_shape=jax.ShapeDtypeStruct((b, l_pad, N_ATTN_HEADS * vd), BF16),
        compiler_params=pltpu.CompilerParams(
            dimension_semantics=("arbitrary", "arbitrary", "arbitrary"),
            vmem_limit_bytes=VMEM_LIMIT),
        name="diffattn",
    )(qk, qk, v, lq1, lk1, lq2, lk2, sub_gain)


def _outproj_kernel(h_ref, f_ref, a_ref, sg_ref, w_ref, o_ref):
    o_ref[...] = _outproj_update(h_ref, f_ref, a_ref, sg_ref, w_ref)


def _outproj(h, f_out, a_out, sg, w_bf):
    rows, d = h.shape
    tm = ROW_TILE
    row_spec = lambda w: pl.BlockSpec((tm, w), lambda i: (i, 0))
    return pl.pallas_call(
        _outproj_kernel,
        grid=(rows // tm,),
        in_specs=[row_spec(d), row_spec(f_out.shape[1]), row_spec(a_out.shape[1]),
                  row_spec(sg.shape[1]), pl.BlockSpec(w_bf.shape, lambda i: (0, 0))],
        out_specs=row_spec(d),
        out_shape=jax.ShapeDtypeStruct((rows, d), F32),
        input_output_aliases={0: 0},
        compiler_params=pltpu.CompilerParams(
            dimension_semantics=("arbitrary",), vmem_limit_bytes=VMEM_LIMIT),
        name="outproj",
    )(h, f_out, a_out, sg, w_bf)


def _assemble_kernel(direct_ref, lo_ref, hi_ref, o_ref, *, anti_diag, direct_end, rev_lo, rev_hi):
    r = o_ref.shape[1]
    t = pl.program_id(1)
    a = t * r + lax.broadcasted_iota(jnp.int32, (r, 1), 0)
    direct = jnp.where(a < direct_end, direct_ref[0], 0.0)

    @pl.when(t < rev_lo // r)
    def _copy():
        o_ref[0] = direct

    @pl.when(t >= rev_lo // r)
    def _reverse():
        src = jnp.concatenate([lo_ref[0], hi_ref[0]], axis=0)
        ri = lax.broadcasted_iota(jnp.int32, (r, 2 * r), 0)
        ci = lax.broadcasted_iota(jnp.int32, (r, 2 * r), 1)
        perm = jnp.where(ri + ci == anti_diag, 1.0, 0.0).astype(BF16)
        p0 = src.astype(BF16)
        r1 = src - p0.astype(F32)
        p1 = r1.astype(BF16)
        p2 = (r1 - p1.astype(F32)).astype(BF16)
        rev = (jnp.dot(perm, p0, preferred_element_type=F32)
               + jnp.dot(perm, p1, preferred_element_type=F32)
               + jnp.dot(perm, p2, preferred_element_type=F32))
        in_rev = jnp.abs(2 * a - (rev_lo + rev_hi - 1)) <= (rev_hi - rev_lo - 1)
        o_ref[0] = jnp.where(in_rev, rev, direct)


def _assemble(direct, src, *, n_rows, direct_end, rev, total):
    b, _, d = direct.shape
    r = ASSEMBLE_TILE
    q0, rem = divmod(total - (r - 1), r)
    n_direct_tiles = -(-direct_end // r)
    n_src_tiles = src.shape[1] // r
    rev_tile = lambda u: jnp.clip(u, 0, n_src_tiles - 1)
    return pl.pallas_call(
        functools.partial(_assemble_kernel, anti_diag=r - 1 + rem, direct_end=direct_end,
                          rev_lo=rev[0], rev_hi=rev[1]),
        grid=(b, n_rows // r),
        in_specs=[pl.BlockSpec((1, r, d), lambda bi, t: (bi, jnp.minimum(t, n_direct_tiles - 1), 0)),
                  pl.BlockSpec((1, r, d), lambda bi, t: (bi, rev_tile(q0 - t), 0)),
                  pl.BlockSpec((1, r, d), lambda bi, t: (bi, rev_tile(q0 - t + 1), 0))],
        out_specs=pl.BlockSpec((1, r, d), lambda bi, t: (bi, t, 0)),
        out_shape=jax.ShapeDtypeStruct((b, n_rows, d), direct.dtype),
        compiler_params=pltpu.CompilerParams(
            dimension_semantics=("arbitrary", "arbitrary"), vmem_limit_bytes=VMEM_LIMIT),
        name="assemble",
    )(direct, src, src)


def _row_positions(l, half):
    n_tok = l // 2 + 1 - N_META
    j = jnp.arange(half, dtype=jnp.int32)
    p = jnp.where(j < n_tok, j + N_META, n_tok + N_META - 1 - j)
    valid_a = j < n_tok + N_META
    valid_b = valid_a & (p != 0) & (2 * p != l)
    pos = jnp.concatenate([p, l - p])
    return jnp.where(jnp.concatenate([valid_a, valid_b]), pos, 0)


def _rope_tables(pos, hd):
    rot = hd // 4
    half = rot // 2
    inv_freq = ROPE_THETA ** (-jnp.arange(0, rot, 2, dtype=F32) / rot)
    ang = pos.astype(F32)[:, None] * inv_freq[None, :]
    cos, sin = jnp.cos(ang), jnp.sin(ang)
    d = jnp.arange(LANES) % hd
    first = d < half
    second = (d >= half) & (d < rot)
    idx = jnp.where(second, d - half, jnp.where(first, d, 0))
    cos_l, sin_l = cos[:, idx], sin[:, idx]
    ct = jnp.where(first | second, cos_l, 1.0)
    sa = jnp.where(first, -sin_l, 0.0)
    sb = jnp.where(second, sin_l, 0.0)
    return ct, sa, sb


def _position_dft(pos, l, half):
    blk = 64
    n_tok = l // 2 + 1 - N_META
    p = pos[:half]
    valid = jnp.arange(half) < n_tok + N_META
    def trig(k):
        ang = ((k[:, None] * p[None, :]) % l).astype(F32) * (2.0 * math.pi / l)
        return jnp.cos(ang), jnp.sin(ang)
    ch, sh = trig(jnp.arange(half // blk, dtype=jnp.int32) * blk)
    cl, sl = trig(jnp.arange(blk, dtype=jnp.int32) + N_META)
    cm = (ch[:, None, :] * cl[None, :, :] - sh[:, None, :] * sl[None, :, :]).reshape(half, half)
    sm = (sh[:, None, :] * cl[None, :, :] + ch[:, None, :] * sl[None, :, :]).reshape(half, half)
    cmeta, smeta = trig(p[n_tok:n_tok + N_META])
    cm = lax.dynamic_update_slice(cm, cmeta, (n_tok, 0))
    sm = lax.dynamic_update_slice(sm, smeta, (n_tok, 0))
    ok = valid[:, None] & valid[None, :]
    return jnp.where(ok, cm, 0.0).astype(BF16), jnp.where(ok, -sm, 0.0).astype(BF16)


def _channel_dft(n):
    c = jnp.arange(n, dtype=jnp.int32)
    ang = ((c[:, None] * c[None, :]) % n).astype(F32) * (2.0 * math.pi / n)
    return jnp.concatenate([jnp.cos(ang), jnp.sin(ang)], axis=1).astype(BF16)


def kernel(x, meta_tokens, norm_gain, w_in, w_fourier, q_norm_gain, k_norm_gain,
           lambda_q1, lambda_k1, lambda_q2, lambda_k2, subln_gain, w_out):
    b, seq, d = x.shape
    depth = w_in.shape[0]
    hd = q_norm_gain.shape[1]
    vd = subln_gain.shape[1]
    fw = w_fourier.shape[1] * w_fourier.shape[2]
    gd = w_fourier.shape[2]
    qkw = N_ATTN_HEADS * 2 * hd
    vw = N_ATTN_HEADS * vd
    l = seq + N_META
    half = _half_len(l)
    l_pad = 2 * half
    n_first = l // 2 + 1
    n_tok = n_first - N_META
    n_pair = n_tok - 1
    assert l % 2 == 0 and N_META < n_first
    assert seq % ASSEMBLE_TILE == 0 and l_pad % ASSEMBLE_TILE == 0
    assert 2 * hd == LANES and vd == LANES and gd == LANES
    assert l_pad % ROW_TILE == 0 and half % DFT_TILE == 0 and l_pad % Q_TILE == 0
    assert half % 64 == 0
    valid_ranges = ((0, n_first), (half, half + n_pair), (half + n_tok, half + n_first - 1))

    pos = _row_positions(l, half)
    ct, sa, sb = _rope_tables(pos, hd)
    cmat, smat = _position_dft(pos, l, half)
    cdft = _channel_dft(gd)
    dft_scale = 1.0 / math.sqrt(l * gd)
    qscale = (hd ** -0.5) * math.log2(math.e)

    h = _assemble(x, x, n_rows=l_pad, direct_end=n_tok, rev=(half, half + n_pair),
                  total=half + seq - N_META)
    meta = jnp.broadcast_to(meta_tokens[::-1][None].astype(x.dtype), (b, N_META, d))
    h = lax.dynamic_update_slice(h, meta, (0, n_tok, 0))
    h = lax.dynamic_update_slice(h, x[:, seq - N_META + 1:], (0, half + n_tok, 0))
    h = h.reshape(b * l_pad, d)

    w_in_bf = w_in.astype(BF16)
    w_out_bf = w_out.astype(BF16)
    w_f_bf = w_fourier.astype(BF16)

    prev = None
    for li in range(depth):
        lambda_init = 0.8 - 0.6 * math.exp(-0.3 * li)
        qk_gain = jnp.concatenate([jnp.tile(q_norm_gain[li], qkw // hd),
                                   jnp.tile(k_norm_gain[li], qkw // hd)])[None]
        outs = _inproj(h, norm_gain[li][None], w_in_bf[li], qk_gain, ct, sa, sb, cdft,
                       l_pad=l_pad, fw=fw, qkw=qkw, vw=vw, hd=hd, qscale=qscale, prev=prev)
        if prev is not None:
            h, outs = outs[0], outs[1:]
        pc, ps, qk, v, sg = outs
        f_out = _fourier(cmat, smat, pc.reshape(b, 2, half, fw), ps.reshape(b, 2, half, fw),
                         w_f_bf[li], scale=dft_scale)
        score_bound = (jnp.max(jnp.abs(q_norm_gain[li])) * jnp.max(jnp.abs(k_norm_gain[li]))
                       * (hd * qscale * 1.02))
        attend = lambda subtract_max: functools.partial(
            _attention, hd=hd, valid_ranges=valid_ranges, lambda_init=lambda_init,
            subtract_max=subtract_max)
        a_out = lax.cond(score_bound <= MAX_UNSHIFTED_SCORE, attend(False), attend(True),
                         qk.reshape(b, l_pad, 2 * qkw), v.reshape(b, l_pad, vw),
                         lambda_q1[li][None], lambda_k1[li][None],
                         lambda_q2[li][None], lambda_k2[li][None], subln_gain[li][None])
        prev = (f_out.reshape(b * l_pad, fw), a_out.reshape(b * l_pad, vw), sg, w_out_bf[li])

    h = _outproj(h, *prev).reshape(b, l_pad, d)
    out = _assemble(h, h, n_rows=seq, direct_end=n_tok, rev=(n_tok, seq - N_META + 1),
                    total=half + seq - N_META)
    return lax.dynamic_update_slice(out, h[:, half + n_tok:half + n_first - 1],
                                    (0, seq - N_META + 1, 0))
```

```python
import functools
import math

import jax
import jax.numpy as jnp
from jax import lax
from jax.experimental import pallas as pl
from jax.experimental.pallas import tpu as pltpu

F32 = jnp.float32
BF16 = jnp.bfloat16

N_META = 16
N_FOURIER_GROUPS = 4
N_ATTN_HEADS = 4
ROPE_THETA = 500000.0
NORM_EPS = 1e-6

MXU_TILE = 256
LANES = 128
VMEM_LIMIT = 56 * 1024 * 1024

ROW_TILE = 544
ROW_SUBTILES = 2
DFT_TILE = 1088
Q_TILE = 256
HEADS_PER_STEP = 4
ASSEMBLE_TILE = 256
MAX_UNSHIFTED_SCORE = 64.0


def _half_len(l):
    return -(-(l // 2 + 1) // LANES) * LANES


def _outproj_update(h_ref, f_ref, a_ref, sg_ref, w_ref):
    fw = f_ref.shape[1]
    sg = sg_ref[...]
    yf = f_ref[...] * sg[:, :fw]
    ya = a_ref[...] * sg[:, fw:]
    return (h_ref[...]
            + jnp.dot(yf, w_ref[:fw, :], preferred_element_type=F32)
            + jnp.dot(ya, w_ref[fw:, :], preferred_element_type=F32))


def _row_subtiles(n_rows):
    n = n_rows // ROW_SUBTILES
    return [pl.ds(i * n, n) for i in range(ROW_SUBTILES)]


def _inproj_kernel(h_ref, g_ref, w_ref, qkg_ref, ct_ref, sa_ref, sb_ref, cdft_ref, *outs, **kw):
    for sub in _row_subtiles(h_ref.shape[0]):
        _inproj_body(h_ref[sub], g_ref, w_ref, qkg_ref, ct_ref.at[sub], sa_ref.at[sub],
                     sb_ref.at[sub], cdft_ref, *[o.at[sub] for o in outs], **kw)


def _outproj_inproj_kernel(h_ref, f_ref, a_ref, sgp_ref, wo_ref, g_ref, w_ref, qkg_ref,
                           ct_ref, sa_ref, sb_ref, cdft_ref, hout_ref, *outs, **kw):
    for sub in _row_subtiles(h_ref.shape[0]):
        h = _outproj_update(h_ref.at[sub], f_ref.at[sub], a_ref.at[sub], sgp_ref.at[sub], wo_ref)
        hout_ref[sub] = h
        _inproj_body(h, g_ref, w_ref, qkg_ref, ct_ref.at[sub], sa_ref.at[sub],
                     sb_ref.at[sub], cdft_ref, *[o.at[sub] for o in outs], **kw)


def _inproj_body(x, g_ref, w_ref, qkg_ref, ct_ref, sa_ref, sb_ref, cdft_ref,
                 pc_ref, ps_ref, qk_ref, v_ref, sg_ref, *, fw, qkw, vw, hd, qscale):
    ms = jnp.mean(x * x, axis=-1, keepdims=True)
    hn = ((x * lax.rsqrt(ms + NORM_EPS)) * g_ref[...]).astype(BF16)

    f = jnp.dot(hn, w_ref[:, 0:fw], preferred_element_type=F32)
    gd = fw // N_FOURIER_GROUPS
    for g in range(N_FOURIER_GROUPS):
        pg = jnp.dot(f[:, g * gd:(g + 1) * gd].astype(BF16), cdft_ref[...],
                     preferred_element_type=F32)
        pc_ref[:, g * gd:(g + 1) * gd] = pg[:, :gd].astype(BF16)
        ps_ref[:, g * gd:(g + 1) * gd] = pg[:, gd:].astype(BF16)

    qk = jnp.dot(hn, w_ref[:, fw:fw + 2 * qkw], preferred_element_type=F32)
    tm = qk.shape[0]
    lo_mask = lax.broadcasted_iota(jnp.int32, (tm, LANES), 1) < hd
    ct = ct_ref[...]
    sa = sa_ref[...]
    sb = sb_ref[...]
    for c in range(2 * qkw // LANES):
        xc = qk[:, c * LANES:(c + 1) * LANES]
        x2 = xc * xc
        lo = jnp.sum(jnp.where(lo_mask, x2, 0.0), axis=-1, keepdims=True)
        hi = jnp.sum(jnp.where(lo_mask, 0.0, x2), axis=-1, keepdims=True)
        msc = jnp.where(lo_mask, lo, hi) * (1.0 / hd)
        y = (xc * lax.rsqrt(msc + NORM_EPS)) * qkg_ref[:, c * LANES:(c + 1) * LANES]
        yr = (y * ct + pltpu.roll(y, LANES - 8, 1) * sa + pltpu.roll(y, 8, 1) * sb)
        if c < qkw // LANES:
            yr = yr * qscale
        qk_ref[:, c * LANES:(c + 1) * LANES] = yr.astype(BF16)

    v = jnp.dot(hn, w_ref[:, fw + 2 * qkw:fw + 2 * qkw + vw], preferred_element_type=F32)
    v_ref[...] = v.astype(BF16)

    gate = jnp.dot(hn, w_ref[:, fw + 2 * qkw + vw:], preferred_element_type=F32)
    sg_ref[...] = (gate * (1.0 / (1.0 + jnp.exp(-gate)))).astype(sg_ref.dtype)


def _inproj(h, gain, w_bf, qk_gain, ct, sa, sb, cdft, *, l_pad, fw, qkw, vw, hd, qscale,
            prev=None):
    rows, d = h.shape
    tm = ROW_TILE
    n_pos_tiles = l_pad // tm
    mixw = w_bf.shape[1] - (fw + 2 * qkw + vw)
    row_spec = lambda w: pl.BlockSpec((tm, w), lambda i: (i, 0))
    const = lambda shape: pl.BlockSpec(shape, lambda i: (0,) * len(shape))
    pos_spec = pl.BlockSpec((tm, LANES), lambda i: (i % n_pos_tiles, 0))
    kw = dict(fw=fw, qkw=qkw, vw=vw, hd=hd, qscale=qscale)
    args = [h, gain, w_bf, qk_gain, ct, sa, sb, cdft]
    in_specs = [row_spec(d), const((1, d)), const(w_bf.shape), const((1, 2 * qkw)),
                pos_spec, pos_spec, pos_spec, const(cdft.shape)]
    out_specs = [row_spec(fw), row_spec(fw), row_spec(2 * qkw), row_spec(vw), row_spec(mixw)]
    out_shape = [jax.ShapeDtypeStruct((rows, fw), BF16),
                 jax.ShapeDtypeStruct((rows, fw), BF16),
                 jax.ShapeDtypeStruct((rows, 2 * qkw), BF16),
                 jax.ShapeDtypeStruct((rows, vw), BF16),
                 jax.ShapeDtypeStruct((rows, mixw), BF16)]
    if prev is None:
        body, name, aliases = functools.partial(_inproj_kernel, **kw), "inproj", {}
    else:
        f_out, a_out, sg_prev, wo_bf = prev
        body, name, aliases = functools.partial(_outproj_inproj_kernel, **kw), "outproj_inproj", {0: 0}
        args[1:1] = [f_out, a_out, sg_prev, wo_bf]
        in_specs[1:1] = [row_spec(f_out.shape[1]), row_spec(a_out.shape[1]),
                         row_spec(sg_prev.shape[1]), const(wo_bf.shape)]
        out_specs.insert(0, row_spec(d))
        out_shape.insert(0, jax.ShapeDtypeStruct((rows, d), F32))
    return pl.pallas_call(
        body,
        grid=(rows // tm,),
        in_specs=in_specs,
        out_specs=out_specs,
        out_shape=out_shape,
        input_output_aliases=aliases,
        compiler_params=pltpu.CompilerParams(
            dimension_semantics=("arbitrary",), vmem_limit_bytes=VMEM_LIMIT),
        name=name,
    )(*args)


def _fourier_kernel(cm_ref, sm_ref, pc_ref, ps_ref, wf_ref, o_ref, pcf_ref, psf_ref, *, scale):
    @pl.when(pl.program_id(1) == 0)
    def _fold():
        pcf_ref[...] = (pc_ref[0, 0].astype(F32) + pc_ref[0, 1].astype(F32)).astype(BF16)
        psf_ref[...] = (ps_ref[0, 0].astype(F32) - ps_ref[0, 1].astype(F32)).astype(BF16)

    even = jnp.dot(cm_ref[...], pcf_ref[...], preferred_element_type=F32) * scale
    odd = jnp.dot(sm_ref[...], psf_ref[...], preferred_element_type=F32) * scale
    gd = even.shape[1] // N_FOURIER_GROUPS
    for half, re in enumerate((even + odd, even - odd)):
        for g in range(N_FOURIER_GROUPS):
            o_ref[0, half, :, g * gd:(g + 1) * gd] = jnp.dot(
                re[:, g * gd:(g + 1) * gd].astype(BF16), wf_ref[g],
                preferred_element_type=F32).astype(o_ref.dtype)


def _fourier(cmat, smat, pc, ps, wf_bf, *, scale):
    b, _, half, fw = pc.shape
    tk = DFT_TILE
    return pl.pallas_call(
        functools.partial(_fourier_kernel, scale=scale),
        grid=(b, half // tk),
        in_specs=[pl.BlockSpec((tk, half), lambda j, i: (i, 0)),
                  pl.BlockSpec((tk, half), lambda j, i: (i, 0)),
                  pl.BlockSpec((1, 2, half, fw), lambda j, i: (j, 0, 0, 0)),
                  pl.BlockSpec((1, 2, half, fw), lambda j, i: (j, 0, 0, 0)),
                  pl.BlockSpec(wf_bf.shape, lambda j, i: (0, 0, 0))],
        out_specs=pl.BlockSpec((1, 2, tk, fw), lambda j, i: (j, 0, i, 0)),
        out_shape=jax.ShapeDtypeStruct((b, 2, half, fw), BF16),
        scratch_shapes=[pltpu.VMEM((half, fw), BF16), pltpu.VMEM((half, fw), BF16)],
        compiler_params=pltpu.CompilerParams(
            dimension_semantics=("arbitrary", "arbitrary"), vmem_limit_bytes=VMEM_LIMIT),
        name="fourier",
    )(cmat, smat, pc, ps, wf_bf)


def _key_segments(l_pad, valid_ranges):
    def clean(lo, hi):
        return any(a <= lo and hi <= b for a, b in valid_ranges)
    segs = []
    for lo in range(0, l_pad, MXU_TILE):
        hi = lo + MXU_TILE
        if segs and segs[-1][2] and clean(lo, hi):
            segs[-1] = (segs[-1][0], hi, True)
        else:
            segs.append((lo, hi, clean(lo, hi)))
    return segs


def _padding_ranges(lo, hi, valid_ranges):
    out, cur = [], lo
    for a, b in sorted(valid_ranges):
        a, b = max(a, lo), min(b, hi)
        if a >= b:
            continue
        if a > cur:
            out.append((cur, a))
        cur = max(cur, b)
    if cur < hi:
        out.append((cur, hi))
    return out


def _attend_head(q, k, v, lam, sub_gain, *, hd, valid_ranges, lambda_init, subtract_max):
    l_pad = k.shape[0]
    tq = q.shape[0]
    lane = lax.broadcasted_iota(jnp.int32, q.shape, 1)
    zero = jnp.zeros_like(q)
    contract_last = (((1,), (1,)), ((), ()))
    contract_first = (((0,), (0,)), ((), ()))

    q2 = jnp.concatenate([jnp.where(lane < hd, q, zero), jnp.where(lane < hd, zero, q)], axis=0)
    s = lax.dot_general(k, q2, contract_last, preferred_element_type=F32)

    segs = _key_segments(l_pad, valid_ranges)
    parts = []
    for lo, hi, is_clean in segs:
        sp = s[lo:hi]
        if not is_clean:
            row2 = 2 * lax.broadcasted_iota(jnp.int32, (hi - lo, 2 * tq), 0)
            for a, b in _padding_ranges(lo, hi, valid_ranges):
                w1 = b - a - 1
                sp = jnp.where(jnp.abs(row2 + (2 * (lo - a) - w1)) <= w1, -jnp.inf, sp)
        parts.append(sp)
    if subtract_max:
        m = functools.reduce(jnp.maximum, [jnp.max(sp, axis=0, keepdims=True) for sp in parts])
        parts = [sp - m for sp in parts]
    parts = [jnp.exp2(sp) for sp in parts]
    l = functools.reduce(jnp.add, [jnp.sum(p, axis=0, keepdims=True) for p in parts])
    o_t = functools.reduce(jnp.add, [
        lax.dot_general(v[lo:hi], p.astype(BF16), contract_first, preferred_element_type=F32)
        for (lo, hi, _), p in zip(segs, parts)]) * (1.0 / l)
    o = (o_t[:, :tq] - lam * o_t[:, tq:]).T
    ms = jnp.mean(o * o, axis=-1, keepdims=True)
    return ((o * lax.rsqrt(ms + NORM_EPS)) * sub_gain) * (1.0 - lambda_init)


def _attn_kernel(q_ref, k_ref, v_ref, lq1_ref, lk1_ref, lq2_ref, lk2_ref, sub_ref, o_ref,
                 *, hd, lambda_init, **kw):
    lam = (jnp.exp(jnp.sum(lq1_ref[...] * lk1_ref[...], axis=-1, keepdims=True))
           - jnp.exp(jnp.sum(lq2_ref[...] * lk2_ref[...], axis=-1, keepdims=True))
           + lambda_init)
    w = 2 * hd
    for i in range(q_ref.shape[2] // w):
        cols = slice(i * w, (i + 1) * w)
        o_ref[0, :, cols] = _attend_head(
            q_ref[0, :, cols], k_ref[0, :, cols], v_ref[0, :, cols], lam, sub_ref[...],
            hd=hd, lambda_init=lambda_init, **kw).astype(o_ref.dtype)


def _attention(qk, v, lq1, lk1, lq2, lk2, sub_gain, *, hd, valid_ranges, lambda_init,
               subtract_max):
    b, l_pad, _ = qk.shape
    vd = 2 * hd
    tq = Q_TILE
    hp = HEADS_PER_STEP
    n_groups = N_ATTN_HEADS // hp
    vec = lambda n: pl.BlockSpec((1, n), lambda bi, h, qi: (0, 0))
    return pl.pallas_call(
        functools.partial(_attn_kernel, hd=hd, valid_ranges=valid_ranges,
                          lambda_init=lambda_init, subtract_max=subtract_max),
        grid=(b, n_groups, l_pad // tq),
        in_specs=[pl.BlockSpec((1, tq, hp * 2 * hd), lambda bi, h, qi: (bi, qi, h)),
                  pl.BlockSpec((1, l_pad, hp * 2 * hd), lambda bi, h, qi: (bi, 0, n_groups + h)),
                  pl.BlockSpec((1, l_pad, hp * vd), lambda bi, h, qi: (bi, 0, h)),
                  vec(hd), vec(hd), vec(hd), vec(hd), vec(vd)],
        out_specs=pl.BlockSpec((1, tq, hp * vd), lambda bi, h, qi: (bi, qi, h)),
        out_shape=jax.ShapeDtypeStruct((b, l_pad, N_ATTN_HEADS * vd), BF16),
        compiler_params=pltpu.CompilerParams(
            dimension_semantics=("arbitrary", "arbitrary", "arbitrary"),
            vmem_limit_bytes=VMEM_LIMIT),
        name="diffattn",
    )(qk, qk, v, lq1, lk1, lq2, lk2, sub_gain)


def _outproj_kernel(h_ref, f_ref, a_ref, sg_ref, w_ref, o_ref):
    o_ref[...] = _outproj_update(h_ref, f_ref, a_ref, sg_ref, w_ref)


def _outproj(h, f_out, a_out, sg, w_bf):
    rows, d = h.shape
    tm = ROW_TILE
    row_spec = lambda w: pl.BlockSpec((tm, w), lambda i: (i, 0))
    return pl.pallas_call(
        _outproj_kernel,
        grid=(rows // tm,),
        in_specs=[row_spec(d), row_spec(f_out.shape[1]), row_spec(a_out.shape[1]),
                  row_spec(sg.shape[1]), pl.BlockSpec(w_bf.shape, lambda i: (0, 0))],
        out_specs=row_spec(d),
        out_shape=jax.ShapeDtypeStruct((rows, d), F32),
        input_output_aliases={0: 0},
        compiler_params=pltpu.CompilerParams(
            dimension_semantics=("arbitrary",), vmem_limit_bytes=VMEM_LIMIT),
        name="outproj",
    )(h, f_out, a_out, sg, w_bf)


def _assemble_kernel(direct_ref, lo_ref, hi_ref, o_ref, *, anti_diag, direct_end, rev_lo, rev_hi):
    r = o_ref.shape[1]
    t = pl.program_id(1)
    a = t * r + lax.broadcasted_iota(jnp.int32, (r, 1), 0)
    direct = jnp.where(a < direct_end, direct_ref[0], 0.0)

    @pl.when(t < rev_lo // r)
    def _copy():
        o_ref[0] = direct

    @pl.when(t >= rev_lo // r)
    def _reverse():
        src = jnp.concatenate([lo_ref[0], hi_ref[0]], axis=0)
        ri = lax.broadcasted_iota(jnp.int32, (r, 2 * r), 0)
        ci = lax.broadcasted_iota(jnp.int32, (r, 2 * r), 1)
        perm = jnp.where(ri + ci == anti_diag, 1.0, 0.0).astype(BF16)
        p0 = src.astype(BF16)
        r1 = src - p0.astype(F32)
        p1 = r1.astype(BF16)
        p2 = (r1 - p1.astype(F32)).astype(BF16)
        rev = (jnp.dot(perm, p0, preferred_element_type=F32)
               + jnp.dot(perm, p1, preferred_element_type=F32)
               + jnp.dot(perm, p2, preferred_element_type=F32))
        in_rev = jnp.abs(2 * a - (rev_lo + rev_hi - 1)) <= (rev_hi - rev_lo - 1)
        o_ref[0] = jnp.where(in_rev, rev, direct)


def _assemble(direct, src, *, n_rows, direct_end, rev, total):
    b, _, d = direct.shape
    r = ASSEMBLE_TILE
    q0, rem = divmod(total - (r - 1), r)
    n_direct_tiles = -(-direct_end // r)
    n_src_tiles = src.shape[1] // r
    rev_tile = lambda u: jnp.clip(u, 0, n_src_tiles - 1)
    return pl.pallas_call(
        functools.partial(_assemble_kernel, anti_diag=r - 1 + rem, direct_end=direct_end,
                          rev_lo=rev[0], rev_hi=rev[1]),
        grid=(b, n_rows // r),
        in_specs=[pl.BlockSpec((1, r, d), lambda bi, t: (bi, jnp.minimum(t, n_direct_tiles - 1), 0)),
                  pl.BlockSpec((1, r, d), lambda bi, t: (bi, rev_tile(q0 - t), 0)),
                  pl.BlockSpec((1, r, d), lambda bi, t: (bi, rev_tile(q0 - t + 1), 0))],
        out_specs=pl.BlockSpec((1, r, d), lambda bi, t: (bi, t, 0)),
        out_shape=jax.ShapeDtypeStruct((b, n_rows, d), direct.dtype),
        compiler_params=pltpu.CompilerParams(
            dimension_semantics=("arbitrary", "arbitrary"), vmem_limit_bytes=VMEM_LIMIT),
        name="assemble",
    )(direct, src, src)


def _row_positions(l, half):
    n_tok = l // 2 + 1 - N_META
    j = jnp.arange(half, dtype=jnp.int32)
    p = jnp.where(j < n_tok, j + N_META, n_tok + N_META - 1 - j)
    valid_a = j < n_tok + N_META
    valid_b = valid_a & (p != 0) & (2 * p != l)
    pos = jnp.concatenate([p, l - p])
    return jnp.where(jnp.concatenate([valid_a, valid_b]), pos, 0)


def _rope_tables(pos, hd):
    rot = hd // 4
    half = rot // 2
    inv_freq = ROPE_THETA ** (-jnp.arange(0, rot, 2, dtype=F32) / rot)
    ang = pos.astype(F32)[:, None] * inv_freq[None, :]
    cos, sin = jnp.cos(ang), jnp.sin(ang)
    d = jnp.arange(LANES) % hd
    first = d < half
    second = (d >= half) & (d < rot)
    idx = jnp.where(second, d - half, jnp.where(first, d, 0))
    cos_l, sin_l = cos[:, idx], sin[:, idx]
    ct = jnp.where(first | second, cos_l, 1.0)
    sa = jnp.where(first, -sin_l, 0.0)
    sb = jnp.where(second, sin_l, 0.0)
    return ct, sa, sb


def _position_dft(pos, l, half):
    blk = 64
    n_tok = l // 2 + 1 - N_META
    p = pos[:half]
    valid = jnp.arange(half) < n_tok + N_META
    def trig(k):
        ang = ((k[:, None] * p[None, :]) % l).astype(F32) * (2.0 * math.pi / l)
        return jnp.cos(ang), jnp.sin(ang)
    ch, sh = trig(jnp.arange(half // blk, dtype=jnp.int32) * blk)
    cl, sl = trig(jnp.arange(blk, dtype=jnp.int32) + N_META)
    cm = (ch[:, None, :] * cl[None, :, :] - sh[:, None, :] * sl[None, :, :]).reshape(half, half)
    sm = (sh[:, None, :] * cl[None, :, :] + ch[:, None, :] * sl[None, :, :]).reshape(half, half)
    cmeta, smeta = trig(p[n_tok:n_tok + N_META])
    cm = lax.dynamic_update_slice(cm, cmeta, (n_tok, 0))
    sm = lax.dynamic_update_slice(sm, smeta, (n_tok, 0))
    ok = valid[:, None] & valid[None, :]
    return jnp.where(ok, cm, 0.0).astype(BF16), jnp.where(ok, -sm, 0.0).astype(BF16)


def _channel_dft(n):
    c = jnp.arange(n, dtype=jnp.int32)
    ang = ((c[:, None] * c[None, :]) % n).astype(F32) * (2.0 * math.pi / n)
    return jnp.concatenate([jnp.cos(ang), jnp.sin(ang)], axis=1).astype(BF16)


def kernel(x, meta_tokens, norm_gain, w_in, w_fourier, q_norm_gain, k_norm_gain,
           lambda_q1, lambda_k1, lambda_q2, lambda_k2, subln_gain, w_out):
    b, seq, d = x.shape
    depth = w_in.shape[0]
    hd = q_norm_gain.shape[1]
    vd = subln_gain.shape[1]
    fw = w_fourier.shape[1] * w_fourier.shape[2]
    gd = w_fourier.shape[2]
    qkw = N_ATTN_HEADS * 2 * hd
    vw = N_ATTN_HEADS * vd
    l = seq + N_META
    half = _half_len(l)
    l_pad = 2 * half
    n_first = l // 2 + 1
    n_tok = n_first - N_META
    n_pair = n_tok - 1
    assert l % 2 == 0 and N_META < n_first
    assert seq % ASSEMBLE_TILE == 0 and l_pad % ASSEMBLE_TILE == 0
    assert 2 * hd == LANES and vd == LANES and gd == LANES
    assert l_pad % ROW_TILE == 0 and half % DFT_TILE == 0 and l_pad % Q_TILE == 0
    assert half % 64 == 0
    valid_ranges = ((0, n_first), (half, half + n_pair), (half + n_tok, half + n_first - 1))

    pos = _row_positions(l, half)
    ct, sa, sb = _rope_tables(pos, hd)
    cmat, smat = _position_dft(pos, l, half)
    cdft = _channel_dft(gd)
    dft_scale = 1.0 / math.sqrt(l * gd)
    qscale = (hd ** -0.5) * math.log2(math.e)

    h = _assemble(x, x, n_rows=l_pad, direct_end=n_tok, rev=(half, half + n_pair),
                  total=half + seq - N_META)
    meta = jnp.broadcast_to(meta_tokens[::-1][None].astype(x.dtype), (b, N_META, d))
    h = lax.dynamic_update_slice(h, meta, (0, n_tok, 0))
    h = lax.dynamic_update_slice(h, x[:, seq - N_META + 1:], (0, half + n_tok, 0))
    h = h.reshape(b * l_pad, d)

    w_in_bf = w_in.astype(BF16)
    w_out_bf = w_out.astype(BF16)
    w_f_bf = w_fourier.astype(BF16)

    prev = None
    for li in range(depth):
        lambda_init = 0.8 - 0.6 * math.exp(-0.3 * li)
        qk_gain = jnp.concatenate([jnp.tile(q_norm_gain[li], qkw // hd),
                                   jnp.tile(k_norm_gain[li], qkw // hd)])[None]
        outs = _inproj(h, norm_gain[li][None], w_in_bf[li], qk_gain, ct, sa, sb, cdft,
                       l_pad=l_pad, fw=fw, qkw=qkw, vw=vw, hd=hd, qscale=qscale, prev=prev)
        if prev is not None:
            h, outs = outs[0], outs[1:]
        pc, ps, qk, v, sg = outs
        f_out = _fourier(cmat, smat, pc.reshape(b, 2, half, fw), ps.reshape(b, 2, half, fw),
                         w_f_bf[li], scale=dft_scale)
        score_bound = (jnp.max(jnp.abs(q_norm_gain[li])) * jnp.max(jnp.abs(k_norm_gain[li]))
                       * (hd * qscale * 1.02))
        attend = lambda subtract_max: functools.partial(
            _attention, hd=hd, valid_ranges=valid_ranges, lambda_init=lambda_init,
            subtract_max=subtract_max)
        a_out = lax.cond(score_bound <= MAX_UNSHIFTED_SCORE, attend(False), attend(True),
                         qk.reshape(b, l_pad, 2 * qkw), v.reshape(b, l_pad, vw),
                         lambda_q1[li][None], lambda_k1[li][None],
                         lambda_q2[li][None], lambda_k2[li][None], subln_gain[li][None])
        prev = (f_out.reshape(b * l_pad, fw), a_out.reshape(b * l_pad, vw), sg, w_out_bf[li])

    h = _outproj(h, *prev).reshape(b, l_pad, d)
    out = _assemble(h, h, n_rows=seq, direct_end=n_tok, rev=(n_tok, seq - N_META + 1),
                    total=half + seq - N_META)
    return lax.dynamic_update_slice(out, h[:, half + n_tok:half + n_first - 1],
                                    (0, seq - N_META + 1, 0))
```

```python
import functools
import math

import jax
import jax.numpy as jnp
from jax import lax
from jax.experimental import pallas as pl
from jax.experimental.pallas import tpu as pltpu

F32 = jnp.float32
BF16 = jnp.bfloat16

N_META = 16
N_FOURIER_GROUPS = 4
N_ATTN_HEADS = 4
ROPE_THETA = 500000.0
NORM_EPS = 1e-6

MXU_TILE = 256
LANES = 128
VMEM_LIMIT = 56 * 1024 * 1024

ROW_TILE = 544
ROW_SUBTILES = 2
DFT_TILE = 1088
Q_TILE = 256
HEADS_PER_STEP = 4
ASSEMBLE_TILE = 256
MAX_UNSHIFTED_SCORE = 64.0


def _half_len(l):
    return -(-(l // 2 + 1) // LANES) * LANES


def _outproj_update(h_ref, f_ref, a_ref, sg_ref, w_ref):
    fw = f_ref.shape[1]
    sg = sg_ref[...]
    yf = f_ref[...] * sg[:, :fw]
    ya = a_ref[...] * sg[:, fw:]
    return (h_ref[...]
            + jnp.dot(yf, w_ref[:fw, :], preferred_element_type=F32)
            + jnp.dot(ya, w_ref[fw:, :], preferred_element_type=F32))


def _row_subtiles(n_rows):
    n = n_rows // ROW_SUBTILES
    return [pl.ds(i * n, n) for i in range(ROW_SUBTILES)]


def _inproj_kernel(h_ref, g_ref, w_ref, qkg_ref, ct_ref, sa_ref, sb_ref, cdft_ref, *outs, **kw):
    for sub in _row_subtiles(h_ref.shape[0]):
        _inproj_body(h_ref[sub], g_ref, w_ref, qkg_ref, ct_ref.at[sub], sa_ref.at[sub],
                     sb_ref.at[sub], cdft_ref, *[o.at[sub] for o in outs], **kw)


def _outproj_inproj_kernel(h_ref, f_ref, a_ref, sgp_ref, wo_ref, g_ref, w_ref, qkg_ref,
                           ct_ref, sa_ref, sb_ref, cdft_ref, hout_ref, *outs, **kw):
    for sub in _row_subtiles(h_ref.shape[0]):
        h = _outproj_update(h_ref.at[sub], f_ref.at[sub], a_ref.at[sub], sgp_ref.at[sub], wo_ref)
        hout_ref[sub] = h
        _inproj_body(h, g_ref, w_ref, qkg_ref, ct_ref.at[sub], sa_ref.at[sub],
                     sb_ref.at[sub], cdft_ref, *[o.at[sub] for o in outs], **kw)


def _inproj_body(x, g_ref, w_ref, qkg_ref, ct_ref, sa_ref, sb_ref, cdft_ref,
                 pc_ref, ps_ref, qk_ref, v_ref, sg_ref, *, fw, qkw, vw, hd, qscale):
    ms = jnp.mean(x * x, axis=-1, keepdims=True)
    hn = ((x * lax.rsqrt(ms + NORM_EPS)) * g_ref[...]).astype(BF16)

    f = jnp.dot(hn, w_ref[:, 0:fw], preferred_element_type=F32)
    gd = fw // N_FOURIER_GROUPS
    for g in range(N_FOURIER_GROUPS):
        pg = jnp.dot(f[:, g * gd:(g + 1) * gd].astype(BF16), cdft_ref[...],
                     preferred_element_type=F32)
        pc_ref[:, g * gd:(g + 1) * gd] = pg[:, :gd].astype(BF16)
        ps_ref[:, g * gd:(g + 1) * gd] = pg[:, gd:].astype(BF16)

    qk = jnp.dot(hn, w_ref[:, fw:fw + 2 * qkw], preferred_element_type=F32)
    tm = qk.shape[0]
    lo_mask = lax.broadcasted_iota(jnp.int32, (tm, LANES), 1) < hd
    ct = ct_ref[...]
    sa = sa_ref[...]
    sb = sb_ref[...]
    for c in range(2 * qkw // LANES):
        xc = qk[:, c * LANES:(c + 1) * LANES]
        x2 = xc * xc
        lo = jnp.sum(jnp.where(lo_mask, x2, 0.0), axis=-1, keepdims=True)
        hi = jnp.sum(jnp.where(lo_mask, 0.0, x2), axis=-1, keepdims=True)
        msc = jnp.where(lo_mask, lo, hi) * (1.0 / hd)
        y = (xc * lax.rsqrt(msc + NORM_EPS)) * qkg_ref[:, c * LANES:(c + 1) * LANES]
        yr = (y * ct + pltpu.roll(y, LANES - 8, 1) * sa + pltpu.roll(y, 8, 1) * sb)
        if c < qkw // LANES:
            yr = yr * qscale
        qk_ref[:, c * LANES:(c + 1) * LANES] = yr.astype(BF16)

    v = jnp.dot(hn, w_ref[:, fw + 2 * qkw:fw + 2 * qkw + vw], preferred_element_type=F32)
    v_ref[...] = v.astype(BF16)

    gate = jnp.dot(hn, w_ref[:, fw + 2 * qkw + vw:], preferred_element_type=F32)
    sg_ref[...] = (gate * (1.0 / (1.0 + jnp.exp(-gate)))).astype(sg_ref.dtype)


def _inproj(h, gain, w_bf, qk_gain, ct, sa, sb, cdft, *, l_pad, fw, qkw, vw, hd, qscale,
            prev=None):
    rows, d = h.shape
    tm = ROW_TILE
    n_pos_tiles = l_pad // tm
    mixw = w_bf.shape[1] - (fw + 2 * qkw + vw)
    row_spec = lambda w: pl.BlockSpec((tm, w), lambda i: (i, 0))
    const = lambda shape: pl.BlockSpec(shape, lambda i: (0,) * len(shape))
    pos_spec = pl.BlockSpec((tm, LANES), lambda i: (i % n_pos_tiles, 0))
    kw = dict(fw=fw, qkw=qkw, vw=vw, hd=hd, qscale=qscale)
    args = [h, gain, w_bf, qk_gain, ct, sa, sb, cdft]
    in_specs = [row_spec(d), const((1, d)), const(w_bf.shape), const((1, 2 * qkw)),
                pos_spec, pos_spec, pos_spec, const(cdft.shape)]
    out_specs = [row_spec(fw), row_spec(fw), row_spec(2 * qkw), row_spec(vw), row_spec(mixw)]
    out_shape = [jax.ShapeDtypeStruct((rows, fw), BF16),
                 jax.ShapeDtypeStruct((rows, fw), BF16),
                 jax.ShapeDtypeStruct((rows, 2 * qkw), BF16),
                 jax.ShapeDtypeStruct((rows, vw), BF16),
                 jax.ShapeDtypeStruct((rows, mixw), BF16)]
    if prev is None:
        body, name, aliases = functools.partial(_inproj_kernel, **kw), "inproj", {}
    else:
        f_out, a_out, sg_prev, wo_bf = prev
        body, name, aliases = functools.partial(_outproj_inproj_kernel, **kw), "outproj_inproj", {0: 0}
        args[1:1] = [f_out, a_out, sg_prev, wo_bf]
        in_specs[1:1] = [row_spec(f_out.shape[1]), row_spec(a_out.shape[1]),
                         row_spec(sg_prev.shape[1]), const(wo_bf.shape)]
        out_specs.insert(0, row_spec(d))
        out_shape.insert(0, jax.ShapeDtypeStruct((rows, d), F32))
    return pl.pallas_call(
        body,
        grid=(rows // tm,),
        in_specs=in_specs,
        out_specs=out_specs,
        out_shape=out_shape,
        input_output_aliases=aliases,
        compiler_params=pltpu.CompilerParams(
            dimension_semantics=("arbitrary",), vmem_limit_bytes=VMEM_LIMIT),
        name=name,
    )(*args)


def _fourier_kernel(cm_ref, sm_ref, pc_ref, ps_ref, wf_ref, o_ref, pcf_ref, psf_ref, *, scale):
    @pl.when(pl.program_id(1) == 0)
    def _fold():
        pcf_ref[...] = (pc_ref[0, 0].astype(F32) + pc_ref[0, 1].astype(F32)).astype(BF16)
        psf_ref[...] = (ps_ref[0, 0].astype(F32) - ps_ref[0, 1].astype(F32)).astype(BF16)

    even = jnp.dot(cm_ref[...], pcf_ref[...], preferred_element_type=F32) * scale
    odd = jnp.dot(sm_ref[...], psf_ref[...], preferred_element_type=F32) * scale
    gd = even.shape[1] // N_FOURIER_GROUPS
    for half, re in enumerate((even + odd, even - odd)):
        for g in range(N_FOURIER_GROUPS):
            o_ref[0, half, :, g * gd:(g + 1) * gd] = jnp.dot(
                re[:, g * gd:(g + 1) * gd].astype(BF16), wf_ref[g],
                preferred_element_type=F32).astype(o_ref.dtype)


def _fourier(cmat, smat, pc, ps, wf_bf, *, scale):
    b, _, half, fw = pc.shape
    tk = DFT_TILE
    return pl.pallas_call(
        functools.partial(_fourier_kernel, scale=scale),
        grid=(b, half // tk),
        in_specs=[pl.BlockSpec((tk, half), lambda j, i: (i, 0)),
                  pl.BlockSpec((tk, half), lambda j, i: (i, 0)),
                  pl.BlockSpec((1, 2, half, fw), lambda j, i: (j, 0, 0, 0)),
                  pl.BlockSpec((1, 2, half, fw), lambda j, i: (j, 0, 0, 0)),
                  pl.BlockSpec(wf_bf.shape, lambda j, i: (0, 0, 0))],
        out_specs=pl.BlockSpec((1, 2, tk, fw), lambda j, i: (j, 0, i, 0)),
        out_shape=jax.ShapeDtypeStruct((b, 2, half, fw), BF16),
        scratch_shapes=[pltpu.VMEM((half, fw), BF16), pltpu.VMEM((half, fw), BF16)],
        compiler_params=pltpu.CompilerParams(
            dimension_semantics=("arbitrary", "arbitrary"), vmem_limit_bytes=VMEM_LIMIT),
        name="fourier",
    )(cmat, smat, pc, ps, wf_bf)


KEY_ALIGN = 16


def _key_plan(valid_ranges):
    covers = []
    for a, b in sorted(valid_ranges):
        lo, hi = a // KEY_ALIGN * KEY_ALIGN, -(-b // KEY_ALIGN) * KEY_ALIGN
        if covers and lo <= covers[-1][1]:
            covers[-1][1] = max(covers[-1][1], hi)
        else:
            covers.append([lo, hi])
    pieces, tail_rows = [], []
    for lo, hi in covers:
        n_full = (hi - lo) // MXU_TILE
        for t in range(n_full):
            a, b = lo + t * MXU_TILE, lo + (t + 1) * MXU_TILE
            pads = [(x - a, y - a) for x, y in _padding_ranges(a, b, valid_ranges)]
            if pieces and not pads and not pieces[-1][1] and pieces[-1][0][-1][1] == a:
                pieces[-1][0][-1] = (pieces[-1][0][-1][0], b)
            else:
                pieces.append(([(a, b)], pads))
        if lo + n_full * MXU_TILE < hi:
            tail_rows.append((lo + n_full * MXU_TILE, hi))
    if tail_rows:
        pads, off = [], 0
        for a, b in tail_rows:
            pads += [(x - a + off, y - a + off) for x, y in _padding_ranges(a, b, valid_ranges)]
            off += b - a
        pieces.append((tail_rows, pads))
    return pieces


def _rows(x, row_ranges):
    parts = [x[a:b] for a, b in row_ranges]
    return parts[0] if len(parts) == 1 else jnp.concatenate(parts, axis=0)


def _padding_ranges(lo, hi, valid_ranges):
    out, cur = [], lo
    for a, b in sorted(valid_ranges):
        a, b = max(a, lo), min(b, hi)
        if a >= b:
            continue
        if a > cur:
            out.append((cur, a))
        cur = max(cur, b)
    if cur < hi:
        out.append((cur, hi))
    return out


def _attend_head(q, k, v, lam, sub_gain, *, hd, valid_ranges, lambda_init, subtract_max):
    tq = q.shape[0]
    lane = lax.broadcasted_iota(jnp.int32, q.shape, 1)
    zero = jnp.zeros_like(q)
    contract_last = (((1,), (1,)), ((), ()))
    contract_first = (((0,), (0,)), ((), ()))

    q2 = jnp.concatenate([jnp.where(lane < hd, q, zero), jnp.where(lane < hd, zero, q)], axis=0)
    plan = _key_plan(valid_ranges)
    parts = []
    for row_ranges, pads in plan:
        sp = lax.dot_general(_rows(k, row_ranges), q2, contract_last,
                             preferred_element_type=F32)
        if pads:
            row2 = 2 * lax.broadcasted_iota(jnp.int32, sp.shape, 0)
            for a, b in pads:
                w1 = b - a - 1
                sp = jnp.where(jnp.abs(row2 - (2 * a + w1)) <= w1, -jnp.inf, sp)
        parts.append(sp)
    if subtract_max:
        m = functools.reduce(jnp.maximum, [jnp.max(sp, axis=0, keepdims=True) for sp in parts])
        parts = [sp - m for sp in parts]
    parts = [jnp.exp2(sp) for sp in parts]
    l = functools.reduce(jnp.add, [jnp.sum(p, axis=0, keepdims=True) for p in parts])
    o_t = functools.reduce(jnp.add, [
        lax.dot_general(_rows(v, row_ranges), p.astype(BF16), contract_first,
                        preferred_element_type=F32)
        for (row_ranges, _), p in zip(plan, parts)]) * (1.0 / l)
    o = (o_t[:, :tq] - lam * o_t[:, tq:]).T
    ms = jnp.mean(o * o, axis=-1, keepdims=True)
    return ((o * lax.rsqrt(ms + NORM_EPS)) * sub_gain) * (1.0 - lambda_init)


def _attn_kernel(q_ref, k_ref, v_ref, lq1_ref, lk1_ref, lq2_ref, lk2_ref, sub_ref, o_ref,
                 *, hd, lambda_init, **kw):
    lam = (jnp.exp(jnp.sum(lq1_ref[...] * lk1_ref[...], axis=-1, keepdims=True))
           - jnp.exp(jnp.sum(lq2_ref[...] * lk2_ref[...], axis=-1, keepdims=True))
           + lambda_init)
    w = 2 * hd
    for i in range(q_ref.shape[2] // w):
        cols = slice(i * w, (i + 1) * w)
        o_ref[0, :, cols] = _attend_head(
            q_ref[0, :, cols], k_ref[0, :, cols], v_ref[0, :, cols], lam, sub_ref[...],
            hd=hd, lambda_init=lambda_init, **kw).astype(o_ref.dtype)


def _attention(qk, v, lq1, lk1, lq2, lk2, sub_gain, *, hd, valid_ranges, lambda_init,
               subtract_max):
    b, l_pad, _ = qk.shape
    vd = 2 * hd
    tq = Q_TILE
    hp = HEADS_PER_STEP
    n_groups = N_ATTN_HEADS // hp
    vec = lambda n: pl.BlockSpec((1, n), lambda bi, h, qi: (0, 0))
    return pl.pallas_call(
        functools.partial(_attn_kernel, hd=hd, valid_ranges=valid_ranges,
                          lambda_init=lambda_init, subtract_max=subtract_max),
        grid=(b, n_groups, l_pad // tq),
        in_specs=[pl.BlockSpec((1, tq, hp * 2 * hd), lambda bi, h, qi: (bi, qi, h)),
                  pl.BlockSpec((1, l_pad, hp * 2 * hd), lambda bi, h, qi: (bi, 0, n_groups + h)),
                  pl.BlockSpec((1, l_pad, hp * vd), lambda bi, h, qi: (bi, 0, h)),
                  vec(hd), vec(hd), vec(hd), vec(hd), vec(vd)],
        out_specs=pl.BlockSpec((1, tq, hp * vd), lambda bi, h, qi: (bi, qi, h)),
        out_shape=jax.ShapeDtypeStruct((b, l_pad, N_ATTN_HEADS * vd), BF16),
        compiler_params=pltpu.CompilerParams(
            dimension_semantics=("arbitrary", "arbitrary", "arbitrary"),
            vmem_limit_bytes=VMEM_LIMIT),
        name="diffattn",
    )(qk, qk, v, lq1, lk1, lq2, lk2, sub_gain)


def _outproj_kernel(h_ref, f_ref, a_ref, sg_ref, w_ref, o_ref):
    o_ref[...] = _outproj_update(h_ref, f_ref, a_ref, sg_ref, w_ref)


def _outproj(h, f_out, a_out, sg, w_bf):
    rows, d = h.shape
    tm = ROW_TILE
    row_spec = lambda w: pl.BlockSpec((tm, w), lambda i: (i, 0))
    return pl.pallas_call(
        _outproj_kernel,
        grid=(rows // tm,),
        in_specs=[row_spec(d), row_spec(f_out.shape[1]), row_spec(a_out.shape[1]),
                  row_spec(sg.shape[1]), pl.BlockSpec(w_bf.shape, lambda i: (0, 0))],
        out_specs=row_spec(d),
        out_shape=jax.ShapeDtypeStruct((rows, d), F32),
        input_output_aliases={0: 0},
        compiler_params=pltpu.CompilerParams(
            dimension_semantics=("arbitrary",), vmem_limit_bytes=VMEM_LIMIT),
        name="outproj",
    )(h, f_out, a_out, sg, w_bf)


def _assemble_kernel(direct_ref, lo_ref, hi_ref, o_ref, *, anti_diag, direct_end, rev_lo, rev_hi):
    r = o_ref.shape[1]
    t = pl.program_id(1)
    a = t * r + lax.broadcasted_iota(jnp.int32, (r, 1), 0)
    direct = jnp.where(a < direct_end, direct_ref[0], 0.0)

    @pl.when(t < rev_lo // r)
    def _copy():
        o_ref[0] = direct

    @pl.when(t >= rev_lo // r)
    def _reverse():
        src = jnp.concatenate([lo_ref[0], hi_ref[0]], axis=0)
        ri = lax.broadcasted_iota(jnp.int32, (r, 2 * r), 0)
        ci = lax.broadcasted_iota(jnp.int32, (r, 2 * r), 1)
        perm = jnp.where(ri + ci == anti_diag, 1.0, 0.0).astype(BF16)
        p0 = src.astype(BF16)
        r1 = src - p0.astype(F32)
        p1 = r1.astype(BF16)
        p2 = (r1 - p1.astype(F32)).astype(BF16)
        rev = (jnp.dot(perm, p0, preferred_element_type=F32)
               + jnp.dot(perm, p1, preferred_element_type=F32)
               + jnp.dot(perm, p2, preferred_element_type=F32))
        in_rev = jnp.abs(2 * a - (rev_lo + rev_hi - 1)) <= (rev_hi - rev_lo - 1)
        o_ref[0] = jnp.where(in_rev, rev, direct)


def _assemble(direct, src, *, n_rows, direct_end, rev, total):
    b, _, d = direct.shape
    r = ASSEMBLE_TILE
    q0, rem = divmod(total - (r - 1), r)
    n_direct_tiles = -(-direct_end // r)
    n_src_tiles = src.shape[1] // r
    rev_tile = lambda u: jnp.clip(u, 0, n_src_tiles - 1)
    return pl.pallas_call(
        functools.partial(_assemble_kernel, anti_diag=r - 1 + rem, direct_end=direct_end,
                          rev_lo=rev[0], rev_hi=rev[1]),
        grid=(b, n_rows // r),
        in_specs=[pl.BlockSpec((1, r, d), lambda bi, t: (bi, jnp.minimum(t, n_direct_tiles - 1), 0)),
                  pl.BlockSpec((1, r, d), lambda bi, t: (bi, rev_tile(q0 - t), 0)),
                  pl.BlockSpec((1, r, d), lambda bi, t: (bi, rev_tile(q0 - t + 1), 0))],
        out_specs=pl.BlockSpec((1, r, d), lambda bi, t: (bi, t, 0)),
        out_shape=jax.ShapeDtypeStruct((b, n_rows, d), direct.dtype),
        compiler_params=pltpu.CompilerParams(
            dimension_semantics=("arbitrary", "arbitrary"), vmem_limit_bytes=VMEM_LIMIT),
        name="assemble",
    )(direct, src, src)


def _row_positions(l, half):
    n_tok = l // 2 + 1 - N_META
    j = jnp.arange(half, dtype=jnp.int32)
    p = jnp.where(j < n_tok, j + N_META, n_tok + N_META - 1 - j)
    valid_a = j < n_tok + N_META
    valid_b = valid_a & (p != 0) & (2 * p != l)
    pos = jnp.concatenate([p, l - p])
    return jnp.where(jnp.concatenate([valid_a, valid_b]), pos, 0)


def _rope_tables(pos, hd):
    rot = hd // 4
    half = rot // 2
    inv_freq = ROPE_THETA ** (-jnp.arange(0, rot, 2, dtype=F32) / rot)
    ang = pos.astype(F32)[:, None] * inv_freq[None, :]
    cos, sin = jnp.cos(ang), jnp.sin(ang)
    d = jnp.arange(LANES) % hd
    first = d < half
    second = (d >= half) & (d < rot)
    idx = jnp.where(second, d - half, jnp.where(first, d, 0))
    cos_l, sin_l = cos[:, idx], sin[:, idx]
    ct = jnp.where(first | second, cos_l, 1.0)
    sa = jnp.where(first, -sin_l, 0.0)
    sb = jnp.where(second, sin_l, 0.0)
    return ct, sa, sb


def _position_dft(pos, l, half):
    blk = 64
    n_tok = l // 2 + 1 - N_META
    p = pos[:half]
    valid = jnp.arange(half) < n_tok + N_META
    def trig(k):
        ang = ((k[:, None] * p[None, :]) % l).astype(F32) * (2.0 * math.pi / l)
        return jnp.cos(ang), jnp.sin(ang)
    ch, sh = trig(jnp.arange(half // blk, dtype=jnp.int32) * blk)
    cl, sl = trig(jnp.arange(blk, dtype=jnp.int32) + N_META)
    cm = (ch[:, None, :] * cl[None, :, :] - sh[:, None, :] * sl[None, :, :]).reshape(half, half)
    sm = (sh[:, None, :] * cl[None, :, :] + ch[:, None, :] * sl[None, :, :]).reshape(half, half)
    cmeta, smeta = trig(p[n_tok:n_tok + N_META])
    cm = lax.dynamic_update_slice(cm, cmeta, (n_tok, 0))
    sm = lax.dynamic_update_slice(sm, smeta, (n_tok, 0))
    ok = valid[:, None] & valid[None, :]
    return jnp.where(ok, cm, 0.0).astype(BF16), jnp.where(ok, -sm, 0.0).astype(BF16)


def _channel_dft(n):
    c = jnp.arange(n, dtype=jnp.int32)
    ang = ((c[:, None] * c[None, :]) % n).astype(F32) * (2.0 * math.pi / n)
    return jnp.concatenate([jnp.cos(ang), jnp.sin(ang)], axis=1).astype(BF16)


def kernel(x, meta_tokens, norm_gain, w_in, w_fourier, q_norm_gain, k_norm_gain,
           lambda_q1, lambda_k1, lambda_q2, lambda_k2, subln_gain, w_out):
    b, seq, d = x.shape
    depth = w_in.shape[0]
    hd = q_norm_gain.shape[1]
    vd = subln_gain.shape[1]
    fw = w_fourier.shape[1] * w_fourier.shape[2]
    gd = w_fourier.shape[2]
    qkw = N_ATTN_HEADS * 2 * hd
    vw = N_ATTN_HEADS * vd
    l = seq + N_META
    half = _half_len(l)
    l_pad = 2 * half
    n_first = l // 2 + 1
    n_tok = n_first - N_META
    n_pair = n_tok - 1
    assert l % 2 == 0 and N_META < n_first
    assert seq % ASSEMBLE_TILE == 0 and l_pad % ASSEMBLE_TILE == 0
    assert 2 * hd == LANES and vd == LANES and gd == LANES
    assert l_pad % ROW_TILE == 0 and half % DFT_TILE == 0 and l_pad % Q_TILE == 0
    assert half % 64 == 0
    valid_ranges = ((0, n_first), (half, half + n_pair), (half + n_tok, half + n_first - 1))

    pos = _row_positions(l, half)
    ct, sa, sb = _rope_tables(pos, hd)
    cmat, smat = _position_dft(pos, l, half)
    cdft = _channel_dft(gd)
    dft_scale = 1.0 / math.sqrt(l * gd)
    qscale = (hd ** -0.5) * math.log2(math.e)

    h = _assemble(x, x, n_rows=l_pad, direct_end=n_tok, rev=(half, half + n_pair),
                  total=half + seq - N_META)
    meta = jnp.broadcast_to(meta_tokens[::-1][None].astype(x.dtype), (b, N_META, d))
    h = lax.dynamic_update_slice(h, meta, (0, n_tok, 0))
    h = lax.dynamic_update_slice(h, x[:, seq - N_META + 1:], (0, half + n_tok, 0))
    h = h.reshape(b * l_pad, d)

    w_in_bf = w_in.astype(BF16)
    w_out_bf = w_out.astype(BF16)
    w_f_bf = w_fourier.astype(BF16)

    prev = None
    for li in range(depth):
        lambda_init = 0.8 - 0.6 * math.exp(-0.3 * li)
        qk_gain = jnp.concatenate([jnp.tile(q_norm_gain[li], qkw // hd),
                                   jnp.tile(k_norm_gain[li], qkw // hd)])[None]
        outs = _inproj(h, norm_gain[li][None], w_in_bf[li], qk_gain, ct, sa, sb, cdft,
                       l_pad=l_pad, fw=fw, qkw=qkw, vw=vw, hd=hd, qscale=qscale, prev=prev)
        if prev is not None:
            h, outs = outs[0], outs[1:]
        pc, ps, qk, v, sg = outs
        f_out = _fourier(cmat, smat, pc.reshape(b, 2, half, fw), ps.reshape(b, 2, half, fw),
                         w_f_bf[li], scale=dft_scale)
        score_bound = (jnp.max(jnp.abs(q_norm_gain[li])) * jnp.max(jnp.abs(k_norm_gain[li]))
                       * (hd * qscale * 1.02))
        attend = lambda subtract_max: functools.partial(
            _attention, hd=hd, valid_ranges=valid_ranges, lambda_init=lambda_init,
            subtract_max=subtract_max)
        a_out = lax.cond(score_bound <= MAX_UNSHIFTED_SCORE, attend(False), attend(True),
                         qk.reshape(b, l_pad, 2 * qkw), v.reshape(b, l_pad, vw),
                         lambda_q1[li][None], lambda_k1[li][None],
                         lambda_q2[li][None], lambda_k2[li][None], subln_gain[li][None])
        prev = (f_out.reshape(b * l_pad, fw), a_out.reshape(b * l_pad, vw), sg, w_out_bf[li])

    h = _outproj(h, *prev).reshape(b, l_pad, d)
    out = _assemble(h, h, n_rows=seq, direct_end=n_tok, rev=(n_tok, seq - N_META + 1),
                    total=half + seq - N_META)
    return lax.dynamic_update_slice(out, h[:, half + n_tok:half + n_first - 1],
                                    (0, seq - N_META + 1, 0))
```

```python
import functools
import math

import jax
import jax.numpy as jnp
from jax import lax
from jax.experimental import pallas as pl
from jax.experimental.pallas import tpu as pltpu

F32 = jnp.float32
BF16 = jnp.bfloat16

N_META = 16
N_FOURIER_GROUPS = 4
N_ATTN_HEADS = 4
ROPE_THETA = 500000.0
NORM_EPS = 1e-6

MXU_TILE = 256
LANES = 128
VMEM_LIMIT = 56 * 1024 * 1024

ROW_TILE = 1088
ROW_SUBTILES = 4
DFT_TILE = 1088
Q_TILE = 256
HEADS_PER_STEP = 4
ASSEMBLE_TILE = 256
MAX_UNSHIFTED_SCORE = 64.0


def _half_len(l):
    return -(-(l // 2 + 1) // LANES) * LANES


def _outproj_update(h_ref, f_ref, a_ref, sg_ref, w_ref):
    fw = f_ref.shape[1]
    sg = sg_ref[...]
    yf = f_ref[...] * sg[:, :fw]
    ya = a_ref[...] * sg[:, fw:]
    return (h_ref[...]
            + jnp.dot(yf, w_ref[:fw, :], preferred_element_type=F32)
            + jnp.dot(ya, w_ref[fw:, :], preferred_element_type=F32))


def _row_subtiles(n_rows):
    n = n_rows // ROW_SUBTILES
    return [pl.ds(i * n, n) for i in range(ROW_SUBTILES)]


def _inproj_kernel(h_ref, g_ref, w_ref, qkg_ref, ct_ref, sa_ref, sb_ref, cdft_ref, *outs, **kw):
    for sub in _row_subtiles(h_ref.shape[0]):
        _inproj_body(h_ref[sub], g_ref, w_ref, qkg_ref, ct_ref.at[sub], sa_ref.at[sub],
                     sb_ref.at[sub], cdft_ref, *[o.at[sub] for o in outs], **kw)


def _outproj_inproj_kernel(h_ref, f_ref, a_ref, sgp_ref, wo_ref, g_ref, w_ref, qkg_ref,
                           ct_ref, sa_ref, sb_ref, cdft_ref, hout_ref, *outs, **kw):
    for sub in _row_subtiles(h_ref.shape[0]):
        h = _outproj_update(h_ref.at[sub], f_ref.at[sub], a_ref.at[sub], sgp_ref.at[sub], wo_ref)
        hout_ref[sub] = h
        _inproj_body(h, g_ref, w_ref, qkg_ref, ct_ref.at[sub], sa_ref.at[sub],
                     sb_ref.at[sub], cdft_ref, *[o.at[sub] for o in outs], **kw)


def _inproj_body(x, g_ref, w_ref, qkg_ref, ct_ref, sa_ref, sb_ref, cdft_ref,
                 pc_ref, ps_ref, qk_ref, v_ref, sg_ref, *, fw, qkw, vw, hd, qscale):
    ms = jnp.mean(x * x, axis=-1, keepdims=True)
    hn = ((x * lax.rsqrt(ms + NORM_EPS)) * g_ref[...]).astype(BF16)

    f = jnp.dot(hn, w_ref[:, 0:fw], preferred_element_type=F32)
    gd = fw // N_FOURIER_GROUPS
    for g in range(N_FOURIER_GROUPS):
        pg = jnp.dot(f[:, g * gd:(g + 1) * gd].astype(BF16), cdft_ref[...],
                     preferred_element_type=F32)
        pc_ref[:, g * gd:(g + 1) * gd] = pg[:, :gd].astype(BF16)
        ps_ref[:, g * gd:(g + 1) * gd] = pg[:, gd:].astype(BF16)

    qk = jnp.dot(hn, w_ref[:, fw:fw + 2 * qkw], preferred_element_type=F32)
    tm = qk.shape[0]
    lo_mask = lax.broadcasted_iota(jnp.int32, (tm, LANES), 1) < hd
    ct = ct_ref[...]
    sa = sa_ref[...]
    sb = sb_ref[...]
    for c in range(2 * qkw // LANES):
        xc = qk[:, c * LANES:(c + 1) * LANES]
        x2 = xc * xc
        lo = jnp.sum(jnp.where(lo_mask, x2, 0.0), axis=-1, keepdims=True)
        hi = jnp.sum(jnp.where(lo_mask, 0.0, x2), axis=-1, keepdims=True)
        msc = jnp.where(lo_mask, lo, hi) * (1.0 / hd)
        y = (xc * lax.rsqrt(msc + NORM_EPS)) * qkg_ref[:, c * LANES:(c + 1) * LANES]
        yr = (y * ct + pltpu.roll(y, LANES - 8, 1) * sa + pltpu.roll(y, 8, 1) * sb)
        if c < qkw // LANES:
            yr = yr * qscale
        qk_ref[:, c * LANES:(c + 1) * LANES] = yr.astype(BF16)

    v = jnp.dot(hn, w_ref[:, fw + 2 * qkw:fw + 2 * qkw + vw], preferred_element_type=F32)
    v_ref[...] = v.astype(BF16)

    gate = jnp.dot(hn, w_ref[:, fw + 2 * qkw + vw:], preferred_element_type=F32)
    sg_ref[...] = (gate * (1.0 / (1.0 + jnp.exp(-gate)))).astype(sg_ref.dtype)


def _inproj(h, gain, w_bf, qk_gain, ct, sa, sb, cdft, *, l_pad, fw, qkw, vw, hd, qscale,
            prev=None):
    rows, d = h.shape
    tm = ROW_TILE
    n_pos_tiles = l_pad // tm
    mixw = w_bf.shape[1] - (fw + 2 * qkw + vw)
    row_spec = lambda w: pl.BlockSpec((tm, w), lambda i: (i, 0))
    const = lambda shape: pl.BlockSpec(shape, lambda i: (0,) * len(shape))
    pos_spec = pl.BlockSpec((tm, LANES), lambda i: (i % n_pos_tiles, 0))
    kw = dict(fw=fw, qkw=qkw, vw=vw, hd=hd, qscale=qscale)
    args = [h, gain, w_bf, qk_gain, ct, sa, sb, cdft]
    resident = lambda shape: pl.BlockSpec(shape, lambda i: (0,) * len(shape),
                                          pipeline_mode=pl.Buffered(1))
    in_specs = [row_spec(d), const((1, d)), resident(w_bf.shape), const((1, 2 * qkw)),
                pos_spec, pos_spec, pos_spec, const(cdft.shape)]
    out_specs = [row_spec(fw), row_spec(fw), row_spec(2 * qkw), row_spec(vw), row_spec(mixw)]
    out_shape = [jax.ShapeDtypeStruct((rows, fw), BF16),
                 jax.ShapeDtypeStruct((rows, fw), BF16),
                 jax.ShapeDtypeStruct((rows, 2 * qkw), BF16),
                 jax.ShapeDtypeStruct((rows, vw), BF16),
                 jax.ShapeDtypeStruct((rows, mixw), BF16)]
    if prev is None:
        body, name, aliases = functools.partial(_inproj_kernel, **kw), "inproj", {}
    else:
        f_out, a_out, sg_prev, wo_bf = prev
        body, name, aliases = functools.partial(_outproj_inproj_kernel, **kw), "outproj_inproj", {0: 0}
        args[1:1] = [f_out, a_out, sg_prev, wo_bf]
        in_specs[1:1] = [row_spec(f_out.shape[1]), row_spec(a_out.shape[1]),
                         row_spec(sg_prev.shape[1]), resident(wo_bf.shape)]
        out_specs.insert(0, row_spec(d))
        out_shape.insert(0, jax.ShapeDtypeStruct((rows, d), F32))
    return pl.pallas_call(
        body,
        grid=(rows // tm,),
        in_specs=in_specs,
        out_specs=out_specs,
        out_shape=out_shape,
        input_output_aliases=aliases,
        compiler_params=pltpu.CompilerParams(
            dimension_semantics=("arbitrary",), vmem_limit_bytes=VMEM_LIMIT),
        name=name,
    )(*args)


def _fourier_kernel(cm_ref, sm_ref, pc_ref, ps_ref, wf_ref, o_ref, pcf_ref, psf_ref, *, scale):
    @pl.when(pl.program_id(1) == 0)
    def _fold():
        pcf_ref[...] = (pc_ref[0, 0].astype(F32) + pc_ref[0, 1].astype(F32)).astype(BF16)
        psf_ref[...] = (ps_ref[0, 0].astype(F32) - ps_ref[0, 1].astype(F32)).astype(BF16)

    even = jnp.dot(cm_ref[...], pcf_ref[...], preferred_element_type=F32) * scale
    odd = jnp.dot(sm_ref[...], psf_ref[...], preferred_element_type=F32) * scale
    gd = even.shape[1] // N_FOURIER_GROUPS
    for half, re in enumerate((even + odd, even - odd)):
        for g in range(N_FOURIER_GROUPS):
            o_ref[0, half, :, g * gd:(g + 1) * gd] = jnp.dot(
                re[:, g * gd:(g + 1) * gd].astype(BF16), wf_ref[g],
                preferred_element_type=F32).astype(o_ref.dtype)


def _fourier(cmat, smat, pc, ps, wf_bf, *, scale):
    b, _, half, fw = pc.shape
    tk = DFT_TILE
    return pl.pallas_call(
        functools.partial(_fourier_kernel, scale=scale),
        grid=(b, half // tk),
        in_specs=[pl.BlockSpec((tk, half), lambda j, i: (i, 0)),
                  pl.BlockSpec((tk, half), lambda j, i: (i, 0)),
                  pl.BlockSpec((1, 2, half, fw), lambda j, i: (j, 0, 0, 0)),
                  pl.BlockSpec((1, 2, half, fw), lambda j, i: (j, 0, 0, 0)),
                  pl.BlockSpec(wf_bf.shape, lambda j, i: (0, 0, 0))],
        out_specs=pl.BlockSpec((1, 2, tk, fw), lambda j, i: (j, 0, i, 0)),
        out_shape=jax.ShapeDtypeStruct((b, 2, half, fw), BF16),
        scratch_shapes=[pltpu.VMEM((half, fw), BF16), pltpu.VMEM((half, fw), BF16)],
        compiler_params=pltpu.CompilerParams(
            dimension_semantics=("arbitrary", "arbitrary"), vmem_limit_bytes=VMEM_LIMIT),
        name="fourier",
    )(cmat, smat, pc, ps, wf_bf)


KEY_ALIGN = 16


def _key_plan(valid_ranges):
    covers = []
    for a, b in sorted(valid_ranges):
        lo, hi = a // KEY_ALIGN * KEY_ALIGN, -(-b // KEY_ALIGN) * KEY_ALIGN
        if covers and lo <= covers[-1][1]:
            covers[-1][1] = max(covers[-1][1], hi)
        else:
            covers.append([lo, hi])
    pieces, tail_rows = [], []
    for lo, hi in covers:
        n_full = (hi - lo) // MXU_TILE
        for t in range(n_full):
            a, b = lo + t * MXU_TILE, lo + (t + 1) * MXU_TILE
            pads = [(x - a, y - a) for x, y in _padding_ranges(a, b, valid_ranges)]
            if pieces and not pads and not pieces[-1][1] and pieces[-1][0][-1][1] == a:
                pieces[-1][0][-1] = (pieces[-1][0][-1][0], b)
            else:
                pieces.append(([(a, b)], pads))
        if lo + n_full * MXU_TILE < hi:
            tail_rows.append((lo + n_full * MXU_TILE, hi))
    if tail_rows:
        pads, off = [], 0
        for a, b in tail_rows:
            pads += [(x - a + off, y - a + off) for x, y in _padding_ranges(a, b, valid_ranges)]
            off += b - a
        pieces.insert(0, (tail_rows, pads))
    return pieces


def _rows(x, row_ranges):
    parts = [x[a:b] for a, b in row_ranges]
    return parts[0] if len(parts) == 1 else jnp.concatenate(parts, axis=0)


def _padding_ranges(lo, hi, valid_ranges):
    out, cur = [], lo
    for a, b in sorted(valid_ranges):
        a, b = max(a, lo), min(b, hi)
        if a >= b:
            continue
        if a > cur:
            out.append((cur, a))
        cur = max(cur, b)
    if cur < hi:
        out.append((cur, hi))
    return out


def _attend_head(q, k, v, lam, sub_gain, *, hd, valid_ranges, lambda_init, subtract_max):
    tq = q.shape[0]
    lane = lax.broadcasted_iota(jnp.int32, q.shape, 1)
    zero = jnp.zeros_like(q)
    contract_last = (((1,), (1,)), ((), ()))
    contract_first = (((0,), (0,)), ((), ()))

    q2 = jnp.concatenate([jnp.where(lane < hd, q, zero), jnp.where(lane < hd, zero, q)], axis=0)
    plan = _key_plan(valid_ranges)
    parts = []
    for row_ranges, pads in plan:
        sp = lax.dot_general(_rows(k, row_ranges), q2, contract_last,
                             preferred_element_type=F32)
        if pads:
            row2 = 2 * lax.broadcasted_iota(jnp.int32, sp.shape, 0)
            for a, b in pads:
                w1 = b - a - 1
                sp = jnp.where(jnp.abs(row2 - (2 * a + w1)) <= w1, -jnp.inf, sp)
        parts.append(sp)
    if subtract_max:
        m = functools.reduce(jnp.maximum, [jnp.max(sp, axis=0, keepdims=True) for sp in parts])
        parts = [sp - m for sp in parts]
    parts = [jnp.exp2(sp) for sp in parts]
    l = functools.reduce(jnp.add, [jnp.sum(p, axis=0, keepdims=True) for p in parts])
    o_t = functools.reduce(jnp.add, [
        lax.dot_general(_rows(v, row_ranges), p.astype(BF16), contract_first,
                        preferred_element_type=F32)
        for (row_ranges, _), p in zip(plan, parts)]) * (1.0 / l)
    o = (o_t[:, :tq] - lam * o_t[:, tq:]).T
    ms = jnp.mean(o * o, axis=-1, keepdims=True)
    return ((o * lax.rsqrt(ms + NORM_EPS)) * sub_gain) * (1.0 - lambda_init)


def _attn_kernel(q_ref, k_ref, v_ref, lq1_ref, lk1_ref, lq2_ref, lk2_ref, sub_ref, o_ref,
                 *, hd, lambda_init, **kw):
    lam = (jnp.exp(jnp.sum(lq1_ref[...] * lk1_ref[...], axis=-1, keepdims=True))
           - jnp.exp(jnp.sum(lq2_ref[...] * lk2_ref[...], axis=-1, keepdims=True))
           + lambda_init)
    w = 2 * hd
    for i in range(q_ref.shape[2] // w):
        cols = slice(i * w, (i + 1) * w)
        o_ref[0, :, cols] = _attend_head(
            q_ref[0, :, cols], k_ref[0, :, cols], v_ref[0, :, cols], lam, sub_ref[...],
            hd=hd, lambda_init=lambda_init, **kw).astype(o_ref.dtype)


def _attention(qk, v, lq1, lk1, lq2, lk2, sub_gain, *, hd, valid_ranges, lambda_init,
               subtract_max):
    b, l_pad, _ = qk.shape
    vd = 2 * hd
    tq = Q_TILE
    hp = HEADS_PER_STEP
    n_groups = N_ATTN_HEADS // hp
    vec = lambda n: pl.BlockSpec((1, n), lambda bi, h, qi: (0, 0))
    return pl.pallas_call(
        functools.partial(_attn_kernel, hd=hd, valid_ranges=valid_ranges,
                          lambda_init=lambda_init, subtract_max=subtract_max),
        grid=(b, n_groups, l_pad // tq),
        in_specs=[pl.BlockSpec((1, tq, hp * 2 * hd), lambda bi, h, qi: (bi, qi, h)),
                  pl.BlockSpec((1, l_pad, hp * 2 * hd), lambda bi, h, qi: (bi, 0, n_groups + h)),
                  pl.BlockSpec((1, l_pad, hp * vd), lambda bi, h, qi: (bi, 0, h)),
                  vec(hd), vec(hd), vec(hd), vec(hd), vec(vd)],
        out_specs=pl.BlockSpec((1, tq, hp * vd), lambda bi, h, qi: (bi, qi, h)),
        out_shape=jax.ShapeDtypeStruct((b, l_pad, N_ATTN_HEADS * vd), BF16),
        compiler_params=pltpu.CompilerParams(
            dimension_semantics=("arbitrary", "arbitrary", "arbitrary"),
            vmem_limit_bytes=VMEM_LIMIT),
        name="diffattn",
    )(qk, qk, v, lq1, lk1, lq2, lk2, sub_gain)


def _outproj_kernel(h_ref, f_ref, a_ref, sg_ref, w_ref, o_ref):
    o_ref[...] = _outproj_update(h_ref, f_ref, a_ref, sg_ref, w_ref)


def _outproj(h, f_out, a_out, sg, w_bf):
    rows, d = h.shape
    tm = ROW_TILE
    row_spec = lambda w: pl.BlockSpec((tm, w), lambda i: (i, 0))
    return pl.pallas_call(
        _outproj_kernel,
        grid=(rows // tm,),
        in_specs=[row_spec(d), row_spec(f_out.shape[1]), row_spec(a_out.shape[1]),
                  row_spec(sg.shape[1]), pl.BlockSpec(w_bf.shape, lambda i: (0, 0))],
        out_specs=row_spec(d),
        out_shape=jax.ShapeDtypeStruct((rows, d), F32),
        input_output_aliases={0: 0},
        compiler_params=pltpu.CompilerParams(
            dimension_semantics=("arbitrary",), vmem_limit_bytes=VMEM_LIMIT),
        name="outproj",
    )(h, f_out, a_out, sg, w_bf)


def _assemble_kernel(direct_ref, lo_ref, hi_ref, o_ref, *, anti_diag, direct_end, rev_lo, rev_hi):
    r = o_ref.shape[1]
    t = pl.program_id(1)
    a = t * r + lax.broadcasted_iota(jnp.int32, (r, 1), 0)
    direct = jnp.where(a < direct_end, direct_ref[0], 0.0)

    @pl.when(t < rev_lo // r)
    def _copy():
        o_ref[0] = direct

    @pl.when(t >= rev_lo // r)
    def _reverse():
        src = jnp.concatenate([lo_ref[0], hi_ref[0]], axis=0)
        ri = lax.broadcasted_iota(jnp.int32, (r, 2 * r), 0)
        ci = lax.broadcasted_iota(jnp.int32, (r, 2 * r), 1)
        perm = jnp.where(ri + ci == anti_diag, 1.0, 0.0).astype(BF16)
        p0 = src.astype(BF16)
        r1 = src - p0.astype(F32)
        p1 = r1.astype(BF16)
        p2 = (r1 - p1.astype(F32)).astype(BF16)
        rev = (jnp.dot(perm, p0, preferred_element_type=F32)
               + jnp.dot(perm, p1, preferred_element_type=F32)
               + jnp.dot(perm, p2, preferred_element_type=F32))
        in_rev = jnp.abs(2 * a - (rev_lo + rev_hi - 1)) <= (rev_hi - rev_lo - 1)
        o_ref[0] = jnp.where(in_rev, rev, direct)


def _assemble(direct, src, *, n_rows, direct_end, rev, total):
    b, _, d = direct.shape
    r = ASSEMBLE_TILE
    q0, rem = divmod(total - (r - 1), r)
    n_direct_tiles = -(-direct_end // r)
    n_src_tiles = src.shape[1] // r
    rev_tile = lambda u: jnp.clip(u, 0, n_src_tiles - 1)
    return pl.pallas_call(
        functools.partial(_assemble_kernel, anti_diag=r - 1 + rem, direct_end=direct_end,
                          rev_lo=rev[0], rev_hi=rev[1]),
        grid=(b, n_rows // r),
        in_specs=[pl.BlockSpec((1, r, d), lambda bi, t: (bi, jnp.minimum(t, n_direct_tiles - 1), 0)),
                  pl.BlockSpec((1, r, d), lambda bi, t: (bi, rev_tile(q0 - t), 0)),
                  pl.BlockSpec((1, r, d), lambda bi, t: (bi, rev_tile(q0 - t + 1), 0))],
        out_specs=pl.BlockSpec((1, r, d), lambda bi, t: (bi, t, 0)),
        out_shape=jax.ShapeDtypeStruct((b, n_rows, d), direct.dtype),
        compiler_params=pltpu.CompilerParams(
            dimension_semantics=("arbitrary", "arbitrary"), vmem_limit_bytes=VMEM_LIMIT),
        name="assemble",
    )(direct, src, src)


def _row_positions(l, half):
    n_tok = l // 2 + 1 - N_META
    j = jnp.arange(half, dtype=jnp.int32)
    p = jnp.where(j < n_tok, j + N_META, n_tok + N_META - 1 - j)
    valid_a = j < n_tok + N_META
    valid_b = valid_a & (p != 0) & (2 * p != l)
    pos = jnp.concatenate([p, l - p])
    return jnp.where(jnp.concatenate([valid_a, valid_b]), pos, 0)


def _rope_tables(pos, hd):
    rot = hd // 4
    half = rot // 2
    inv_freq = ROPE_THETA ** (-jnp.arange(0, rot, 2, dtype=F32) / rot)
    ang = pos.astype(F32)[:, None] * inv_freq[None, :]
    cos, sin = jnp.cos(ang), jnp.sin(ang)
    d = jnp.arange(LANES) % hd
    first = d < half
    second = (d >= half) & (d < rot)
    idx = jnp.where(second, d - half, jnp.where(first, d, 0))
    cos_l, sin_l = cos[:, idx], sin[:, idx]
    ct = jnp.where(first | second, cos_l, 1.0)
    sa = jnp.where(first, -sin_l, 0.0)
    sb = jnp.where(second, sin_l, 0.0)
    return ct, sa, sb


def _position_dft(pos, l, half):
    blk = 64
    n_tok = l // 2 + 1 - N_META
    p = pos[:half]
    valid = jnp.arange(half) < n_tok + N_META
    def trig(k):
        ang = ((k[:, None] * p[None, :]) % l).astype(F32) * (2.0 * math.pi / l)
        return jnp.cos(ang), jnp.sin(ang)
    ch, sh = trig(jnp.arange(half // blk, dtype=jnp.int32) * blk)
    cl, sl = trig(jnp.arange(blk, dtype=jnp.int32) + N_META)
    cm = (ch[:, None, :] * cl[None, :, :] - sh[:, None, :] * sl[None, :, :]).reshape(half, half)
    sm = (sh[:, None, :] * cl[None, :, :] + ch[:, None, :] * sl[None, :, :]).reshape(half, half)
    cmeta, smeta = trig(p[n_tok:n_tok + N_META])
    cm = lax.dynamic_update_slice(cm, cmeta, (n_tok, 0))
    sm = lax.dynamic_update_slice(sm, smeta, (n_tok, 0))
    ok = valid[:, None] & valid[None, :]
    return jnp.where(ok, cm, 0.0).astype(BF16), jnp.where(ok, -sm, 0.0).astype(BF16)


def _channel_dft(n):
    c = jnp.arange(n, dtype=jnp.int32)
    ang = ((c[:, None] * c[None, :]) % n).astype(F32) * (2.0 * math.pi / n)
    return jnp.concatenate([jnp.cos(ang), jnp.sin(ang)], axis=1).astype(BF16)


def kernel(x, meta_tokens, norm_gain, w_in, w_fourier, q_norm_gain, k_norm_gain,
           lambda_q1, lambda_k1, lambda_q2, lambda_k2, subln_gain, w_out):
    b, seq, d = x.shape
    depth = w_in.shape[0]
    hd = q_norm_gain.shape[1]
    vd = subln_gain.shape[1]
    fw = w_fourier.shape[1] * w_fourier.shape[2]
    gd = w_fourier.shape[2]
    qkw = N_ATTN_HEADS * 2 * hd
    vw = N_ATTN_HEADS * vd
    l = seq + N_META
    half = _half_len(l)
    l_pad = 2 * half
    n_first = l // 2 + 1
    n_tok = n_first - N_META
    n_pair = n_tok - 1
    assert l % 2 == 0 and N_META < n_first
    assert seq % ASSEMBLE_TILE == 0 and l_pad % ASSEMBLE_TILE == 0
    assert 2 * hd == LANES and vd == LANES and gd == LANES
    assert l_pad % ROW_TILE == 0 and half % DFT_TILE == 0 and l_pad % Q_TILE == 0
    assert half % 64 == 0
    valid_ranges = ((0, n_first), (half, half + n_pair), (half + n_tok, half + n_first - 1))

    pos = _row_positions(l, half)
    ct, sa, sb = _rope_tables(pos, hd)
    cmat, smat = _position_dft(pos, l, half)
    cdft = _channel_dft(gd)
    dft_scale = 1.0 / math.sqrt(l * gd)
    qscale = (hd ** -0.5) * math.log2(math.e)

    h = _assemble(x, x, n_rows=l_pad, direct_end=n_tok, rev=(half, half + n_pair),
                  total=half + seq - N_META)
    meta = jnp.broadcast_to(meta_tokens[::-1][None].astype(x.dtype), (b, N_META, d))
    h = lax.dynamic_update_slice(h, meta, (0, n_tok, 0))
    h = lax.dynamic_update_slice(h, x[:, seq - N_META + 1:], (0, half + n_tok, 0))
    h = h.reshape(b * l_pad, d)

    w_in_bf = w_in.astype(BF16)
    w_out_bf = w_out.astype(BF16)
    w_f_bf = w_fourier.astype(BF16)

    prev = None
    for li in range(depth):
        lambda_init = 0.8 - 0.6 * math.exp(-0.3 * li)
        qk_gain = jnp.concatenate([jnp.tile(q_norm_gain[li], qkw // hd),
                                   jnp.tile(k_norm_gain[li], qkw // hd)])[None]
        outs = _inproj(h, norm_gain[li][None], w_in_bf[li], qk_gain, ct, sa, sb, cdft,
                       l_pad=l_pad, fw=fw, qkw=qkw, vw=vw, hd=hd, qscale=qscale, prev=prev)
        if prev is not None:
            h, outs = outs[0], outs[1:]
        pc, ps, qk, v, sg = outs
        f_out = _fourier(cmat, smat, pc.reshape(b, 2, half, fw), ps.reshape(b, 2, half, fw),
                         w_f_bf[li], scale=dft_scale)
        score_bound = (jnp.max(jnp.abs(q_norm_gain[li])) * jnp.max(jnp.abs(k_norm_gain[li]))
                       * (hd * qscale * 1.02))
        attend = lambda subtract_max: functools.partial(
            _attention, hd=hd, valid_ranges=valid_ranges, lambda_init=lambda_init,
            subtract_max=subtract_max)
        a_out = lax.cond(score_bound <= MAX_UNSHIFTED_SCORE, attend(False), attend(True),
                         qk.reshape(b, l_pad, 2 * qkw), v.reshape(b, l_pad, vw),
                         lambda_q1[li][None], lambda_k1[li][None],
                         lambda_q2[li][None], lambda_k2[li][None], subln_gain[li][None])
        prev = (f_out.reshape(b * l_pad, fw), a_out.reshape(b * l_pad, vw), sg, w_out_bf[li])

    h = _outproj(h, *prev).reshape(b, l_pad, d)
    out = _assemble(h, h, n_rows=seq, direct_end=n_tok, rev=(n_tok, seq - N_META + 1),
                    total=half + seq - N_META)
    return lax.dynamic_update_slice(out, h[:, half + n_tok:half + n_first - 1],
                                    (0, seq - N_META + 1, 0))
```

```python
import functools
import math

import jax
import jax.numpy as jnp
from jax import lax
from jax.experimental import pallas as pl
from jax.experimental.pallas import tpu as pltpu

F32 = jnp.float32
BF16 = jnp.bfloat16

N_META = 16
N_FOURIER_GROUPS = 4
N_ATTN_HEADS = 4
ROPE_THETA = 500000.0
NORM_EPS = 1e-6

MXU_TILE = 256
LANES = 128
VMEM_LIMIT = 56 * 1024 * 1024

ROW_TILE = 1088
FUSED_ROW_TILE = 544
ROW_SUBTILE = 272
DFT_TILE = 1088
Q_TILE = 256
HEADS_PER_STEP = 4
ASSEMBLE_TILE = 256
MAX_UNSHIFTED_SCORE = 64.0


def _half_len(l):
    return -(-(l // 2 + 1) // LANES) * LANES


def _outproj_update(h_ref, f_ref, a_ref, sg_ref, w_ref):
    fw = f_ref.shape[1]
    sg = sg_ref[...]
    yf = f_ref[...] * sg[:, :fw]
    ya = a_ref[...] * sg[:, fw:]
    return (h_ref[...]
            + jnp.dot(yf, w_ref[:fw, :], preferred_element_type=F32)
            + jnp.dot(ya, w_ref[fw:, :], preferred_element_type=F32))


def _row_subtiles(n_rows):
    return [pl.ds(i * ROW_SUBTILE, ROW_SUBTILE) for i in range(n_rows // ROW_SUBTILE)]


def _inproj_kernel(h_ref, g_ref, w_ref, qkg_ref, ct_ref, sa_ref, sb_ref, cdft_ref, *outs, **kw):
    for sub in _row_subtiles(h_ref.shape[0]):
        _inproj_body(h_ref[sub], g_ref, w_ref, qkg_ref, ct_ref.at[sub], sa_ref.at[sub],
                     sb_ref.at[sub], cdft_ref, *[o.at[sub] for o in outs], **kw)


def _outproj_inproj_kernel(h_ref, f_ref, a_ref, sgp_ref, wo_ref, g_ref, w_ref, qkg_ref,
                           ct_ref, sa_ref, sb_ref, cdft_ref, hout_ref, *outs, **kw):
    for sub in _row_subtiles(h_ref.shape[0]):
        h = _outproj_update(h_ref.at[sub], f_ref.at[sub], a_ref.at[sub], sgp_ref.at[sub], wo_ref)
        hout_ref[sub] = h
        _inproj_body(h, g_ref, w_ref, qkg_ref, ct_ref.at[sub], sa_ref.at[sub],
                     sb_ref.at[sub], cdft_ref, *[o.at[sub] for o in outs], **kw)


def _inproj_body(x, g_ref, w_ref, qkg_ref, ct_ref, sa_ref, sb_ref, cdft_ref,
                 pc_ref, ps_ref, qk_ref, v_ref, sg_ref, *, fw, qkw, vw, hd, qscale):
    ms = jnp.mean(x * x, axis=-1, keepdims=True)
    hn = ((x * lax.rsqrt(ms + NORM_EPS)) * g_ref[...]).astype(BF16)

    f = jnp.dot(hn, w_ref[:, 0:fw], preferred_element_type=F32)
    gd = fw // N_FOURIER_GROUPS
    for g in range(N_FOURIER_GROUPS):
        pg = jnp.dot(f[:, g * gd:(g + 1) * gd].astype(BF16), cdft_ref[...],
                     preferred_element_type=F32)
        pc_ref[:, g * gd:(g + 1) * gd] = pg[:, :gd].astype(BF16)
        ps_ref[:, g * gd:(g + 1) * gd] = pg[:, gd:].astype(BF16)

    qk = jnp.dot(hn, w_ref[:, fw:fw + 2 * qkw], preferred_element_type=F32)
    tm = qk.shape[0]
    lo_mask = lax.broadcasted_iota(jnp.int32, (tm, LANES), 1) < hd
    ct = ct_ref[...]
    sa = sa_ref[...]
    sb = sb_ref[...]
    for c in range(2 * qkw // LANES):
        xc = qk[:, c * LANES:(c + 1) * LANES]
        x2 = xc * xc
        lo = jnp.sum(jnp.where(lo_mask, x2, 0.0), axis=-1, keepdims=True)
        hi = jnp.sum(jnp.where(lo_mask, 0.0, x2), axis=-1, keepdims=True)
        msc = jnp.where(lo_mask, lo, hi) * (1.0 / hd)
        y = (xc * lax.rsqrt(msc + NORM_EPS)) * qkg_ref[:, c * LANES:(c + 1) * LANES]
        yr = (y * ct + pltpu.roll(y, LANES - 8, 1) * sa + pltpu.roll(y, 8, 1) * sb)
        if c < qkw // LANES:
            yr = yr * qscale
        qk_ref[:, c * LANES:(c + 1) * LANES] = yr.astype(BF16)

    v = jnp.dot(hn, w_ref[:, fw + 2 * qkw:fw + 2 * qkw + vw], preferred_element_type=F32)
    v_ref[...] = v.astype(BF16)

    gate = jnp.dot(hn, w_ref[:, fw + 2 * qkw + vw:], preferred_element_type=F32)
    sg_ref[...] = (gate * (1.0 / (1.0 + jnp.exp(-gate)))).astype(sg_ref.dtype)


def _inproj(h, gain, w_bf, qk_gain, ct, sa, sb, cdft, *, l_pad, fw, qkw, vw, hd, qscale,
            prev=None):
    rows, d = h.shape
    tm = ROW_TILE if prev is None else FUSED_ROW_TILE
    n_pos_tiles = l_pad // tm
    mixw = w_bf.shape[1] - (fw + 2 * qkw + vw)
    row_spec = lambda w: pl.BlockSpec((tm, w), lambda i: (i, 0))
    const = lambda shape: pl.BlockSpec(shape, lambda i: (0,) * len(shape))
    pos_spec = pl.BlockSpec((tm, LANES), lambda i: (i % n_pos_tiles, 0))
    kw = dict(fw=fw, qkw=qkw, vw=vw, hd=hd, qscale=qscale)
    args = [h, gain, w_bf, qk_gain, ct, sa, sb, cdft]
    resident = lambda shape: pl.BlockSpec(shape, lambda i: (0,) * len(shape),
                                          pipeline_mode=pl.Buffered(1))
    in_specs = [row_spec(d), const((1, d)), resident(w_bf.shape), const((1, 2 * qkw)),
                pos_spec, pos_spec, pos_spec, const(cdft.shape)]
    out_specs = [row_spec(fw), row_spec(fw), row_spec(2 * qkw), row_spec(vw), row_spec(mixw)]
    out_shape = [jax.ShapeDtypeStruct((rows, fw), BF16),
                 jax.ShapeDtypeStruct((rows, fw), BF16),
                 jax.ShapeDtypeStruct((rows, 2 * qkw), BF16),
                 jax.ShapeDtypeStruct((rows, vw), BF16),
                 jax.ShapeDtypeStruct((rows, mixw), BF16)]
    if prev is None:
        body, name, aliases = functools.partial(_inproj_kernel, **kw), "inproj", {}
    else:
        f_out, a_out, sg_prev, wo_bf = prev
        body, name, aliases = functools.partial(_outproj_inproj_kernel, **kw), "outproj_inproj", {0: 0}
        args[1:1] = [f_out, a_out, sg_prev, wo_bf]
        in_specs[1:1] = [row_spec(f_out.shape[1]), row_spec(a_out.shape[1]),
                         row_spec(sg_prev.shape[1]), resident(wo_bf.shape)]
        out_specs.insert(0, row_spec(d))
        out_shape.insert(0, jax.ShapeDtypeStruct((rows, d), F32))
    return pl.pallas_call(
        body,
        grid=(rows // tm,),
        in_specs=in_specs,
        out_specs=out_specs,
        out_shape=out_shape,
        input_output_aliases=aliases,
        compiler_params=pltpu.CompilerParams(
            dimension_semantics=("arbitrary",), vmem_limit_bytes=VMEM_LIMIT),
        name=name,
    )(*args)


def _fourier_kernel(cm_ref, sm_ref, pc_ref, ps_ref, wf_ref, o_ref, pcf_ref, psf_ref, *, scale):
    @pl.when(pl.program_id(1) == 0)
    def _fold():
        pcf_ref[...] = (pc_ref[0, 0].astype(F32) + pc_ref[0, 1].astype(F32)).astype(BF16)
        psf_ref[...] = (ps_ref[0, 0].astype(F32) - ps_ref[0, 1].astype(F32)).astype(BF16)

    even = jnp.dot(cm_ref[...], pcf_ref[...], preferred_element_type=F32) * scale
    odd = jnp.dot(sm_ref[...], psf_ref[...], preferred_element_type=F32) * scale
    gd = even.shape[1] // N_FOURIER_GROUPS
    for half, re in enumerate((even + odd, even - odd)):
        for g in range(N_FOURIER_GROUPS):
            o_ref[0, half, :, g * gd:(g + 1) * gd] = jnp.dot(
                re[:, g * gd:(g + 1) * gd].astype(BF16), wf_ref[g],
                preferred_element_type=F32).astype(o_ref.dtype)


def _fourier(cmat, smat, pc, ps, wf_bf, *, scale):
    b, _, half, fw = pc.shape
    tk = DFT_TILE
    return pl.pallas_call(
        functools.partial(_fourier_kernel, scale=scale),
        grid=(b, half // tk),
        in_specs=[pl.BlockSpec((tk, half), lambda j, i: (i, 0)),
                  pl.BlockSpec((tk, half), lambda j, i: (i, 0)),
                  pl.BlockSpec((1, 2, half, fw), lambda j, i: (j, 0, 0, 0)),
                  pl.BlockSpec((1, 2, half, fw), lambda j, i: (j, 0, 0, 0)),
                  pl.BlockSpec(wf_bf.shape, lambda j, i: (0, 0, 0))],
        out_specs=pl.BlockSpec((1, 2, tk, fw), lambda j, i: (j, 0, i, 0)),
        out_shape=jax.ShapeDtypeStruct((b, 2, half, fw), BF16),
        scratch_shapes=[pltpu.VMEM((half, fw), BF16), pltpu.VMEM((half, fw), BF16)],
        compiler_params=pltpu.CompilerParams(
            dimension_semantics=("arbitrary", "arbitrary"), vmem_limit_bytes=VMEM_LIMIT),
        name="fourier",
    )(cmat, smat, pc, ps, wf_bf)


KEY_ALIGN = 16


def _key_plan(valid_ranges):
    covers = []
    for a, b in sorted(valid_ranges):
        lo, hi = a // KEY_ALIGN * KEY_ALIGN, -(-b // KEY_ALIGN) * KEY_ALIGN
        if covers and lo <= covers[-1][1]:
            covers[-1][1] = max(covers[-1][1], hi)
        else:
            covers.append([lo, hi])
    pieces, tail_rows = [], []
    for lo, hi in covers:
        n_full = (hi - lo) // MXU_TILE
        for t in range(n_full):
            a, b = lo + t * MXU_TILE, lo + (t + 1) * MXU_TILE
            pads = [(x - a, y - a) for x, y in _padding_ranges(a, b, valid_ranges)]
            if pieces and not pads and not pieces[-1][1] and pieces[-1][0][-1][1] == a:
                pieces[-1][0][-1] = (pieces[-1][0][-1][0], b)
            else:
                pieces.append(([(a, b)], pads))
        if lo + n_full * MXU_TILE < hi:
            tail_rows.append((lo + n_full * MXU_TILE, hi))
    if tail_rows:
        pads, off = [], 0
        for a, b in tail_rows:
            pads += [(x - a + off, y - a + off) for x, y in _padding_ranges(a, b, valid_ranges)]
            off += b - a
        pieces.insert(0, (tail_rows, pads))
    return pieces


def _rows(x, row_ranges):
    parts = [x[a:b] for a, b in row_ranges]
    return parts[0] if len(parts) == 1 else jnp.concatenate(parts, axis=0)


def _padding_ranges(lo, hi, valid_ranges):
    out, cur = [], lo
    for a, b in sorted(valid_ranges):
        a, b = max(a, lo), min(b, hi)
        if a >= b:
            continue
        if a > cur:
            out.append((cur, a))
        cur = max(cur, b)
    if cur < hi:
        out.append((cur, hi))
    return out


def _attend_head(q, k, v, lam, sub_gain, *, hd, valid_ranges, lambda_init, subtract_max):
    tq = q.shape[0]
    lane = lax.broadcasted_iota(jnp.int32, q.shape, 1)
    zero = jnp.zeros_like(q)
    contract_last = (((1,), (1,)), ((), ()))
    contract_first = (((0,), (0,)), ((), ()))

    q2 = jnp.concatenate([jnp.where(lane < hd, q, zero), jnp.where(lane < hd, zero, q)], axis=0)
    plan = _key_plan(valid_ranges)
    parts = []
    for row_ranges, pads in plan:
        sp = lax.dot_general(_rows(k, row_ranges), q2, contract_last,
                             preferred_element_type=F32)
        if pads:
            row2 = 2 * lax.broadcasted_iota(jnp.int32, sp.shape, 0)
            for a, b in pads:
                w1 = b - a - 1
                sp = jnp.where(jnp.abs(row2 - (2 * a + w1)) <= w1, -jnp.inf, sp)
        parts.append(sp)
    if subtract_max:
        m = functools.reduce(jnp.maximum, [jnp.max(sp, axis=0, keepdims=True) for sp in parts])
        parts = [sp - m for sp in parts]
    parts = [jnp.exp2(sp) for sp in parts]
    l = functools.reduce(jnp.add, [jnp.sum(p, axis=0, keepdims=True) for p in parts])
    o_t = functools.reduce(jnp.add, [
        lax.dot_general(_rows(v, row_ranges), p.astype(BF16), contract_first,
                        preferred_element_type=F32)
        for (row_ranges, _), p in zip(plan, parts)]) * (1.0 / l)
    o = (o_t[:, :tq] - lam * o_t[:, tq:]).T
    ms = jnp.mean(o * o, axis=-1, keepdims=True)
    return ((o * lax.rsqrt(ms + NORM_EPS)) * sub_gain) * (1.0 - lambda_init)


def _attn_kernel(q_ref, k_ref, v_ref, lq1_ref, lk1_ref, lq2_ref, lk2_ref, sub_ref, o_ref,
                 *, hd, lambda_init, **kw):
    lam = (jnp.exp(jnp.sum(lq1_ref[...] * lk1_ref[...], axis=-1, keepdims=True))
           - jnp.exp(jnp.sum(lq2_ref[...] * lk2_ref[...], axis=-1, keepdims=True))
           + lambda_init)
    w = 2 * hd
    for i in range(q_ref.shape[2] // w):
        cols = slice(i * w, (i + 1) * w)
        o_ref[0, :, cols] = _attend_head(
            q_ref[0, :, cols], k_ref[0, :, cols], v_ref[0, :, cols], lam, sub_ref[...],
            hd=hd, lambda_init=lambda_init, **kw).astype(o_ref.dtype)


def _attention(qk, v, lq1, lk1, lq2, lk2, sub_gain, *, hd, valid_ranges, lambda_init,
               subtract_max):
    b, l_pad, _ = qk.shape
    vd = 2 * hd
    tq = Q_TILE
    hp = HEADS_PER_STEP
    n_groups = N_ATTN_HEADS // hp
    vec = lambda n: pl.BlockSpec((1, n), lambda bi, h, qi: (0, 0))
    return pl.pallas_call(
        functools.partial(_attn_kernel, hd=hd, valid_ranges=valid_ranges,
                          lambda_init=lambda_init, subtract_max=subtract_max),
        grid=(b, n_groups, l_pad // tq),
        in_specs=[pl.BlockSpec((1, tq, hp * 2 * hd), lambda bi, h, qi: (bi, qi, h)),
                  pl.BlockSpec((1, l_pad, hp * 2 * hd), lambda bi, h, qi: (bi, 0, n_groups + h)),
                  pl.BlockSpec((1, l_pad, hp * vd), lambda bi, h, qi: (bi, 0, h)),
                  vec(hd), vec(hd), vec(hd), vec(hd), vec(vd)],
        out_specs=pl.BlockSpec((1, tq, hp * vd), lambda bi, h, qi: (bi, qi, h)),
        out_shape=jax.ShapeDtypeStruct((b, l_pad, N_ATTN_HEADS * vd), BF16),
        compiler_params=pltpu.CompilerParams(
            dimension_semantics=("arbitrary", "arbitrary", "arbitrary"),
            vmem_limit_bytes=VMEM_LIMIT),
        name="diffattn",
    )(qk, qk, v, lq1, lk1, lq2, lk2, sub_gain)


def _outproj_kernel(h_ref, f_ref, a_ref, sg_ref, w_ref, o_ref):
    o_ref[...] = _outproj_update(h_ref, f_ref, a_ref, sg_ref, w_ref)


def _outproj(h, f_out, a_out, sg, w_bf):
    rows, d = h.shape
    tm = ROW_TILE
    row_spec = lambda w: pl.BlockSpec((tm, w), lambda i: (i, 0))
    return pl.pallas_call(
        _outproj_kernel,
        grid=(rows // tm,),
        in_specs=[row_spec(d), row_spec(f_out.shape[1]), row_spec(a_out.shape[1]),
                  row_spec(sg.shape[1]), pl.BlockSpec(w_bf.shape, lambda i: (0, 0))],
        out_specs=row_spec(d),
        out_shape=jax.ShapeDtypeStruct((rows, d), F32),
        input_output_aliases={0: 0},
        compiler_params=pltpu.CompilerParams(
            dimension_semantics=("arbitrary",), vmem_limit_bytes=VMEM_LIMIT),
        name="outproj",
    )(h, f_out, a_out, sg, w_bf)


def _assemble_kernel(direct_ref, lo_ref, hi_ref, o_ref, *, anti_diag, direct_end, rev_lo, rev_hi):
    r = o_ref.shape[1]
    t = pl.program_id(1)
    a = t * r + lax.broadcasted_iota(jnp.int32, (r, 1), 0)
    direct = jnp.where(a < direct_end, direct_ref[0], 0.0)

    @pl.when(t < rev_lo // r)
    def _copy():
        o_ref[0] = direct

    @pl.when(t >= rev_lo // r)
    def _reverse():
        src = jnp.concatenate([lo_ref[0], hi_ref[0]], axis=0)
        ri = lax.broadcasted_iota(jnp.int32, (r, 2 * r), 0)
        ci = lax.broadcasted_iota(jnp.int32, (r, 2 * r), 1)
        perm = jnp.where(ri + ci == anti_diag, 1.0, 0.0).astype(BF16)
        p0 = src.astype(BF16)
        r1 = src - p0.astype(F32)
        p1 = r1.astype(BF16)
        p2 = (r1 - p1.astype(F32)).astype(BF16)
        rev = (jnp.dot(perm, p0, preferred_element_type=F32)
               + jnp.dot(perm, p1, preferred_element_type=F32)
               + jnp.dot(perm, p2, preferred_element_type=F32))
        in_rev = jnp.abs(2 * a - (rev_lo + rev_hi - 1)) <= (rev_hi - rev_lo - 1)
        o_ref[0] = jnp.where(in_rev, rev, direct)


def _assemble(direct, src, *, n_rows, direct_end, rev, total):
    b, _, d = direct.shape
    r = ASSEMBLE_TILE
    q0, rem = divmod(total - (r - 1), r)
    n_direct_tiles = -(-direct_end // r)
    n_src_tiles = src.shape[1] // r
    rev_tile = lambda u: jnp.clip(u, 0, n_src_tiles - 1)
    return pl.pallas_call(
        functools.partial(_assemble_kernel, anti_diag=r - 1 + rem, direct_end=direct_end,
                          rev_lo=rev[0], rev_hi=rev[1]),
        grid=(b, n_rows // r),
        in_specs=[pl.BlockSpec((1, r, d), lambda bi, t: (bi, jnp.minimum(t, n_direct_tiles - 1), 0)),
                  pl.BlockSpec((1, r, d), lambda bi, t: (bi, rev_tile(q0 - t), 0)),
                  pl.BlockSpec((1, r, d), lambda bi, t: (bi, rev_tile(q0 - t + 1), 0))],
        out_specs=pl.BlockSpec((1, r, d), lambda bi, t: (bi, t, 0)),
        out_shape=jax.ShapeDtypeStruct((b, n_rows, d), direct.dtype),
        compiler_params=pltpu.CompilerParams(
            dimension_semantics=("arbitrary", "arbitrary"), vmem_limit_bytes=VMEM_LIMIT),
        name="assemble",
    )(direct, src, src)


def _row_positions(l, half):
    n_tok = l // 2 + 1 - N_META
    j = jnp.arange(half, dtype=jnp.int32)
    p = jnp.where(j < n_tok, j + N_META, n_tok + N_META - 1 - j)
    valid_a = j < n_tok + N_META
    valid_b = valid_a & (p != 0) & (2 * p != l)
    pos = jnp.concatenate([p, l - p])
    return jnp.where(jnp.concatenate([valid_a, valid_b]), pos, 0)


def _rope_tables(pos, hd):
    rot = hd // 4
    half = rot // 2
    inv_freq = ROPE_THETA ** (-jnp.arange(0, rot, 2, dtype=F32) / rot)
    ang = pos.astype(F32)[:, None] * inv_freq[None, :]
    cos, sin = jnp.cos(ang), jnp.sin(ang)
    d = jnp.arange(LANES) % hd
    first = d < half
    second = (d >= half) & (d < rot)
    idx = jnp.where(second, d - half, jnp.where(first, d, 0))
    cos_l, sin_l = cos[:, idx], sin[:, idx]
    ct = jnp.where(first | second, cos_l, 1.0)
    sa = jnp.where(first, -sin_l, 0.0)
    sb = jnp.where(second, sin_l, 0.0)
    return ct, sa, sb


def _position_dft(pos, l, half):
    blk = 64
    n_tok = l // 2 + 1 - N_META
    p = pos[:half]
    valid = jnp.arange(half) < n_tok + N_META
    def trig(k):
        ang = ((k[:, None] * p[None, :]) % l).astype(F32) * (2.0 * math.pi / l)
        return jnp.cos(ang), jnp.sin(ang)
    ch, sh = trig(jnp.arange(half // blk, dtype=jnp.int32) * blk)
    cl, sl = trig(jnp.arange(blk, dtype=jnp.int32) + N_META)
    cm = (ch[:, None, :] * cl[None, :, :] - sh[:, None, :] * sl[None, :, :]).reshape(half, half)
    sm = (sh[:, None, :] * cl[None, :, :] + ch[:, None, :] * sl[None, :, :]).reshape(half, half)
    cmeta, smeta = trig(p[n_tok:n_tok + N_META])
    cm = lax.dynamic_update_slice(cm, cmeta, (n_tok, 0))
    sm = lax.dynamic_update_slice(sm, smeta, (n_tok, 0))
    ok = valid[:, None] & valid[None, :]
    return jnp.where(ok, cm, 0.0).astype(BF16), jnp.where(ok, -sm, 0.0).astype(BF16)


def _channel_dft(n):
    c = jnp.arange(n, dtype=jnp.int32)
    ang = ((c[:, None] * c[None, :]) % n).astype(F32) * (2.0 * math.pi / n)
    return jnp.concatenate([jnp.cos(ang), jnp.sin(ang)], axis=1).astype(BF16)


def kernel(x, meta_tokens, norm_gain, w_in, w_fourier, q_norm_gain, k_norm_gain,
           lambda_q1, lambda_k1, lambda_q2, lambda_k2, subln_gain, w_out):
    b, seq, d = x.shape
    depth = w_in.shape[0]
    hd = q_norm_gain.shape[1]
    vd = subln_gain.shape[1]
    fw = w_fourier.shape[1] * w_fourier.shape[2]
    gd = w_fourier.shape[2]
    qkw = N_ATTN_HEADS * 2 * hd
    vw = N_ATTN_HEADS * vd
    l = seq + N_META
    half = _half_len(l)
    l_pad = 2 * half
    n_first = l // 2 + 1
    n_tok = n_first - N_META
    n_pair = n_tok - 1
    assert l % 2 == 0 and N_META < n_first
    assert seq % ASSEMBLE_TILE == 0 and l_pad % ASSEMBLE_TILE == 0
    assert 2 * hd == LANES and vd == LANES and gd == LANES
    assert l_pad % ROW_TILE == 0 and l_pad % FUSED_ROW_TILE == 0
    assert ROW_TILE % ROW_SUBTILE == 0 and FUSED_ROW_TILE % ROW_SUBTILE == 0
    assert half % DFT_TILE == 0 and l_pad % Q_TILE == 0
    assert half % 64 == 0
    valid_ranges = ((0, n_first), (half, half + n_pair), (half + n_tok, half + n_first - 1))

    pos = _row_positions(l, half)
    ct, sa, sb = _rope_tables(pos, hd)
    cmat, smat = _position_dft(pos, l, half)
    cdft = _channel_dft(gd)
    dft_scale = 1.0 / math.sqrt(l * gd)
    qscale = (hd ** -0.5) * math.log2(math.e)

    h = _assemble(x, x, n_rows=l_pad, direct_end=n_tok, rev=(half, half + n_pair),
                  total=half + seq - N_META)
    meta = jnp.broadcast_to(meta_tokens[::-1][None].astype(x.dtype), (b, N_META, d))
    h = lax.dynamic_update_slice(h, meta, (0, n_tok, 0))
    h = lax.dynamic_update_slice(h, x[:, seq - N_META + 1:], (0, half + n_tok, 0))
    h = h.reshape(b * l_pad, d)

    w_in_bf = w_in.astype(BF16)
    w_out_bf = w_out.astype(BF16)
    w_f_bf = w_fourier.astype(BF16)

    prev = None
    for li in range(depth):
        lambda_init = 0.8 - 0.6 * math.exp(-0.3 * li)
        qk_gain = jnp.concatenate([jnp.tile(q_norm_gain[li], qkw // hd),
                                   jnp.tile(k_norm_gain[li], qkw // hd)])[None]
        outs = _inproj(h, norm_gain[li][None], w_in_bf[li], qk_gain, ct, sa, sb, cdft,
                       l_pad=l_pad, fw=fw, qkw=qkw, vw=vw, hd=hd, qscale=qscale, prev=prev)
        if prev is not None:
            h, outs = outs[0], outs[1:]
        pc, ps, qk, v, sg = outs
        f_out = _fourier(cmat, smat, pc.reshape(b, 2, half, fw), ps.reshape(b, 2, half, fw),
                         w_f_bf[li], scale=dft_scale)
        score_bound = (jnp.max(jnp.abs(q_norm_gain[li])) * jnp.max(jnp.abs(k_norm_gain[li]))
                       * (hd * qscale * 1.02))
        attend = lambda subtract_max: functools.partial(
            _attention, hd=hd, valid_ranges=valid_ranges, lambda_init=lambda_init,
            subtract_max=subtract_max)
        a_out = lax.cond(score_bound <= MAX_UNSHIFTED_SCORE, attend(False), attend(True),
                         qk.reshape(b, l_pad, 2 * qkw), v.reshape(b, l_pad, vw),
                         lambda_q1[li][None], lambda_k1[li][None],
                         lambda_q2[li][None], lambda_k2[li][None], subln_gain[li][None])
        prev = (f_out.reshape(b * l_pad, fw), a_out.reshape(b * l_pad, vw), sg, w_out_bf[li])

    h = _outproj(h, *prev).reshape(b, l_pad, d)
    out = _assemble(h, h, n_rows=seq, direct_end=n_tok, rev=(n_tok, seq - N_META + 1),
                    total=half + seq - N_META)
    return lax.dynamic_update_slice(out, h[:, half + n_tok:half + n_first - 1],
                                    (0, seq - N_META + 1, 0))
```

```python
import functools
import math

import jax
import jax.numpy as jnp
from jax import lax
from jax.experimental import pallas as pl
from jax.experimental.pallas import tpu as pltpu

F32 = jnp.float32
BF16 = jnp.bfloat16

N_META = 16
N_FOURIER_GROUPS = 4
N_ATTN_HEADS = 4
ROPE_THETA = 500000.0
NORM_EPS = 1e-6

MXU_TILE = 256
LANES = 128
VMEM_LIMIT = 56 * 1024 * 1024

ROW_TILE = 1088
FUSED_ROW_TILE = 544
ROW_SUBTILE = 272
DFT_TILE = 1088
Q_TILE = 256
HEADS_PER_STEP = 4
ASSEMBLE_TILE = 256
MAX_UNSHIFTED_SCORE = 64.0


def _half_len(l):
    return -(-(l // 2 + 1) // LANES) * LANES


def _outproj_update(h_ref, f_ref, a_ref, sg_ref, w_ref):
    fw = f_ref.shape[1]
    sg = sg_ref[...]
    yf = f_ref[...] * sg[:, :fw]
    ya = a_ref[...] * sg[:, fw:]
    return (h_ref[...]
            + jnp.dot(yf, w_ref[:fw, :], preferred_element_type=F32)
            + jnp.dot(ya, w_ref[fw:, :], preferred_element_type=F32))


def _row_subtiles(n_rows):
    return [pl.ds(i * ROW_SUBTILE, ROW_SUBTILE) for i in range(n_rows // ROW_SUBTILE)]


def _inproj_kernel(h_ref, g_ref, w_ref, qkg_ref, ct_ref, sa_ref, sb_ref, cdft_ref, *outs, **kw):
    for sub in _row_subtiles(h_ref.shape[0]):
        _inproj_body(h_ref[sub], g_ref, w_ref, qkg_ref, ct_ref.at[sub], sa_ref.at[sub],
                     sb_ref.at[sub], cdft_ref, *[o.at[sub] for o in outs], **kw)


def _outproj_inproj_kernel(h_ref, f_ref, a_ref, sgp_ref, wo_ref, g_ref, w_ref, qkg_ref,
                           ct_ref, sa_ref, sb_ref, cdft_ref, hout_ref, *outs, **kw):
    for sub in _row_subtiles(h_ref.shape[0]):
        h = _outproj_update(h_ref.at[sub], f_ref.at[sub], a_ref.at[sub], sgp_ref.at[sub], wo_ref)
        hout_ref[sub] = h
        _inproj_body(h, g_ref, w_ref, qkg_ref, ct_ref.at[sub], sa_ref.at[sub],
                     sb_ref.at[sub], cdft_ref, *[o.at[sub] for o in outs], **kw)


def _inproj_body(x, g_ref, w_ref, qkg_ref, ct_ref, sa_ref, sb_ref, cdft_ref,
                 pc_ref, ps_ref, qk_ref, v_ref, sg_ref, *, fw, qkw, vw, hd, qscale):
    ms = jnp.mean(x * x, axis=-1, keepdims=True)
    hn = ((x * lax.rsqrt(ms + NORM_EPS)) * g_ref[...]).astype(BF16)

    f = jnp.dot(hn, w_ref[:, 0:fw], preferred_element_type=F32)
    gd = fw // N_FOURIER_GROUPS
    for g in range(N_FOURIER_GROUPS):
        pg = jnp.dot(f[:, g * gd:(g + 1) * gd].astype(BF16), cdft_ref[...],
                     preferred_element_type=F32)
        pc_ref[:, g * gd:(g + 1) * gd] = pg[:, :gd].astype(BF16)
        ps_ref[:, g * gd:(g + 1) * gd] = pg[:, gd:].astype(BF16)

    qk = jnp.dot(hn, w_ref[:, fw:fw + 2 * qkw], preferred_element_type=F32)
    tm = qk.shape[0]
    lo_mask = lax.broadcasted_iota(jnp.int32, (tm, LANES), 1) < hd
    ct = ct_ref[...]
    sa = sa_ref[...]
    sb = sb_ref[...]
    for c in range(2 * qkw // LANES):
        xc = qk[:, c * LANES:(c + 1) * LANES]
        x2 = xc * xc
        lo = jnp.sum(jnp.where(lo_mask, x2, 0.0), axis=-1, keepdims=True)
        hi = jnp.sum(jnp.where(lo_mask, 0.0, x2), axis=-1, keepdims=True)
        msc = jnp.where(lo_mask, lo, hi) * (1.0 / hd)
        y = (xc * lax.rsqrt(msc + NORM_EPS)) * qkg_ref[:, c * LANES:(c + 1) * LANES]
        yr = (y * ct + pltpu.roll(y, LANES - 8, 1) * sa + pltpu.roll(y, 8, 1) * sb)
        if c < qkw // LANES:
            yr = yr * qscale
        qk_ref[:, c * LANES:(c + 1) * LANES] = yr.astype(BF16)

    v = jnp.dot(hn, w_ref[:, fw + 2 * qkw:fw + 2 * qkw + vw], preferred_element_type=F32)
    v_ref[...] = v.astype(BF16)

    gate = jnp.dot(hn, w_ref[:, fw + 2 * qkw + vw:], preferred_element_type=F32)
    sg_ref[...] = (gate * (1.0 / (1.0 + jnp.exp(-gate)))).astype(sg_ref.dtype)


def _layer_spec(w, layer):
    return pl.BlockSpec((None,) + w.shape[1:], lambda i: (layer,) + (0,) * (w.ndim - 1),
                        pipeline_mode=pl.Buffered(1))


def _inproj(h, gain, w_bf, qk_gain, ct, sa, sb, cdft, *, layer, l_pad, fw, qkw, vw, hd, qscale,
            prev=None):
    rows, d = h.shape
    tm = ROW_TILE if prev is None else FUSED_ROW_TILE
    n_pos_tiles = l_pad // tm
    mixw = w_bf.shape[2] - (fw + 2 * qkw + vw)
    row_spec = lambda w: pl.BlockSpec((tm, w), lambda i: (i, 0))
    const = lambda shape: pl.BlockSpec(shape, lambda i: (0,) * len(shape))
    pos_spec = pl.BlockSpec((tm, LANES), lambda i: (i % n_pos_tiles, 0))
    kw = dict(fw=fw, qkw=qkw, vw=vw, hd=hd, qscale=qscale)
    args = [h, gain, w_bf, qk_gain, ct, sa, sb, cdft]
    in_specs = [row_spec(d), const((1, d)), _layer_spec(w_bf, layer), const((1, 2 * qkw)),
                pos_spec, pos_spec, pos_spec, const(cdft.shape)]
    out_specs = [row_spec(fw), row_spec(fw), row_spec(2 * qkw), row_spec(vw), row_spec(mixw)]
    out_shape = [jax.ShapeDtypeStruct((rows, fw), BF16),
                 jax.ShapeDtypeStruct((rows, fw), BF16),
                 jax.ShapeDtypeStruct((rows, 2 * qkw), BF16),
                 jax.ShapeDtypeStruct((rows, vw), BF16),
                 jax.ShapeDtypeStruct((rows, mixw), BF16)]
    if prev is None:
        body, name, aliases = functools.partial(_inproj_kernel, **kw), "inproj", {}
    else:
        f_out, a_out, sg_prev, wo_bf = prev
        body, name, aliases = functools.partial(_outproj_inproj_kernel, **kw), "outproj_inproj", {0: 0}
        args[1:1] = [f_out, a_out, sg_prev, wo_bf]
        in_specs[1:1] = [row_spec(f_out.shape[1]), row_spec(a_out.shape[1]),
                         row_spec(sg_prev.shape[1]), _layer_spec(wo_bf, layer - 1)]
        out_specs.insert(0, row_spec(d))
        out_shape.insert(0, jax.ShapeDtypeStruct((rows, d), F32))
    return pl.pallas_call(
        body,
        grid=(rows // tm,),
        in_specs=in_specs,
        out_specs=out_specs,
        out_shape=out_shape,
        input_output_aliases=aliases,
        compiler_params=pltpu.CompilerParams(
            dimension_semantics=("arbitrary",), vmem_limit_bytes=VMEM_LIMIT),
        name=name,
    )(*args)


def _fourier_kernel(cm_ref, sm_ref, pc_ref, ps_ref, wf_ref, o_ref, pcf_ref, psf_ref, *, scale):
    @pl.when(pl.program_id(1) == 0)
    def _fold():
        pcf_ref[...] = (pc_ref[0, 0].astype(F32) + pc_ref[0, 1].astype(F32)).astype(BF16)
        psf_ref[...] = (ps_ref[0, 0].astype(F32) - ps_ref[0, 1].astype(F32)).astype(BF16)

    even = jnp.dot(cm_ref[...], pcf_ref[...], preferred_element_type=F32) * scale
    odd = jnp.dot(sm_ref[...], psf_ref[...], preferred_element_type=F32) * scale
    gd = even.shape[1] // N_FOURIER_GROUPS
    for half, re in enumerate((even + odd, even - odd)):
        for g in range(N_FOURIER_GROUPS):
            o_ref[0, half, :, g * gd:(g + 1) * gd] = jnp.dot(
                re[:, g * gd:(g + 1) * gd].astype(BF16), wf_ref[g],
                preferred_element_type=F32).astype(o_ref.dtype)


def _fourier(cmat, smat, pc, ps, wf_bf, *, scale):
    b, _, half, fw = pc.shape
    tk = DFT_TILE
    return pl.pallas_call(
        functools.partial(_fourier_kernel, scale=scale),
        grid=(b, half // tk),
        in_specs=[pl.BlockSpec((tk, half), lambda j, i: (i, 0)),
                  pl.BlockSpec((tk, half), lambda j, i: (i, 0)),
                  pl.BlockSpec((1, 2, half, fw), lambda j, i: (j, 0, 0, 0)),
                  pl.BlockSpec((1, 2, half, fw), lambda j, i: (j, 0, 0, 0)),
                  pl.BlockSpec(wf_bf.shape, lambda j, i: (0, 0, 0))],
        out_specs=pl.BlockSpec((1, 2, tk, fw), lambda j, i: (j, 0, i, 0)),
        out_shape=jax.ShapeDtypeStruct((b, 2, half, fw), BF16),
        scratch_shapes=[pltpu.VMEM((half, fw), BF16), pltpu.VMEM((half, fw), BF16)],
        compiler_params=pltpu.CompilerParams(
            dimension_semantics=("arbitrary", "arbitrary"), vmem_limit_bytes=VMEM_LIMIT),
        name="fourier",
    )(cmat, smat, pc, ps, wf_bf)


KEY_ALIGN = 16


def _key_plan(valid_ranges):
    covers = []
    for a, b in sorted(valid_ranges):
        lo, hi = a // KEY_ALIGN * KEY_ALIGN, -(-b // KEY_ALIGN) * KEY_ALIGN
        if covers and lo <= covers[-1][1]:
            covers[-1][1] = max(covers[-1][1], hi)
        else:
            covers.append([lo, hi])
    pieces, tail_rows = [], []
    for lo, hi in covers:
        n_full = (hi - lo) // MXU_TILE
        for t in range(n_full):
            a, b = lo + t * MXU_TILE, lo + (t + 1) * MXU_TILE
            pads = [(x - a, y - a) for x, y in _padding_ranges(a, b, valid_ranges)]
            if pieces and not pads and not pieces[-1][1] and pieces[-1][0][-1][1] == a:
                pieces[-1][0][-1] = (pieces[-1][0][-1][0], b)
            else:
                pieces.append(([(a, b)], pads))
        if lo + n_full * MXU_TILE < hi:
            tail_rows.append((lo + n_full * MXU_TILE, hi))
    if tail_rows:
        pads, off = [], 0
        for a, b in tail_rows:
            pads += [(x - a + off, y - a + off) for x, y in _padding_ranges(a, b, valid_ranges)]
            off += b - a
        pieces.insert(0, (tail_rows, pads))
    return pieces


def _rows(x, row_ranges):
    parts = [x[a:b] for a, b in row_ranges]
    return parts[0] if len(parts) == 1 else jnp.concatenate(parts, axis=0)


def _padding_ranges(lo, hi, valid_ranges):
    out, cur = [], lo
    for a, b in sorted(valid_ranges):
        a, b = max(a, lo), min(b, hi)
        if a >= b:
            continue
        if a > cur:
            out.append((cur, a))
        cur = max(cur, b)
    if cur < hi:
        out.append((cur, hi))
    return out


def _attend_head(q, k, v, lam, sub_gain, *, hd, valid_ranges, lambda_init, subtract_max):
    tq = q.shape[0]
    lane = lax.broadcasted_iota(jnp.int32, q.shape, 1)
    zero = jnp.zeros_like(q)
    contract_last = (((1,), (1,)), ((), ()))
    contract_first = (((0,), (0,)), ((), ()))

    q2 = jnp.concatenate([jnp.where(lane < hd, q, zero), jnp.where(lane < hd, zero, q)], axis=0)
    plan = _key_plan(valid_ranges)
    parts = []
    for row_ranges, pads in plan:
        sp = lax.dot_general(_rows(k, row_ranges), q2, contract_last,
                             preferred_element_type=F32)
        if pads:
            row2 = 2 * lax.broadcasted_iota(jnp.int32, sp.shape, 0)
            for a, b in pads:
                w1 = b - a - 1
                sp = jnp.where(jnp.abs(row2 - (2 * a + w1)) <= w1, -jnp.inf, sp)
        parts.append(sp)
    if subtract_max:
        m = functools.reduce(jnp.maximum, [jnp.max(sp, axis=0, keepdims=True) for sp in parts])
        parts = [sp - m for sp in parts]
    parts = [jnp.exp2(sp) for sp in parts]
    l = functools.reduce(jnp.add, [jnp.sum(p, axis=0, keepdims=True) for p in parts])
    o_t = functools.reduce(jnp.add, [
        lax.dot_general(_rows(v, row_ranges), p.astype(BF16), contract_first,
                        preferred_element_type=F32)
        for (row_ranges, _), p in zip(plan, parts)]) * (1.0 / l)
    o = (o_t[:, :tq] - lam * o_t[:, tq:]).T
    ms = jnp.mean(o * o, axis=-1, keepdims=True)
    return ((o * lax.rsqrt(ms + NORM_EPS)) * sub_gain) * (1.0 - lambda_init)


def _attn_kernel(q_ref, k_ref, v_ref, lq1_ref, lk1_ref, lq2_ref, lk2_ref, sub_ref, o_ref,
                 *, hd, lambda_init, **kw):
    lam = (jnp.exp(jnp.sum(lq1_ref[...] * lk1_ref[...], axis=-1, keepdims=True))
           - jnp.exp(jnp.sum(lq2_ref[...] * lk2_ref[...], axis=-1, keepdims=True))
           + lambda_init)
    w = 2 * hd
    for i in range(q_ref.shape[2] // w):
        cols = slice(i * w, (i + 1) * w)
        o_ref[0, :, cols] = _attend_head(
            q_ref[0, :, cols], k_ref[0, :, cols], v_ref[0, :, cols], lam, sub_ref[...],
            hd=hd, lambda_init=lambda_init, **kw).astype(o_ref.dtype)


def _attention(qk, v, lq1, lk1, lq2, lk2, sub_gain, *, hd, valid_ranges, lambda_init,
               subtract_max):
    b, l_pad, _ = qk.shape
    vd = 2 * hd
    tq = Q_TILE
    hp = HEADS_PER_STEP
    n_groups = N_ATTN_HEADS // hp
    vec = lambda n: pl.BlockSpec((1, n), lambda bi, h, qi: (0, 0))
    return pl.pallas_call(
        functools.partial(_attn_kernel, hd=hd, valid_ranges=valid_ranges,
                          lambda_init=lambda_init, subtract_max=subtract_max),
        grid=(b, n_groups, l_pad // tq),
        in_specs=[pl.BlockSpec((1, tq, hp * 2 * hd), lambda bi, h, qi: (bi, qi, h)),
                  pl.BlockSpec((1, l_pad, hp * 2 * hd), lambda bi, h, qi: (bi, 0, n_groups + h)),
                  pl.BlockSpec((1, l_pad, hp * vd), lambda bi, h, qi: (bi, 0, h)),
                  vec(hd), vec(hd), vec(hd), vec(hd), vec(vd)],
        out_specs=pl.BlockSpec((1, tq, hp * vd), lambda bi, h, qi: (bi, qi, h)),
        out_shape=jax.ShapeDtypeStruct((b, l_pad, N_ATTN_HEADS * vd), BF16),
        compiler_params=pltpu.CompilerParams(
            dimension_semantics=("arbitrary", "arbitrary", "arbitrary"),
            vmem_limit_bytes=VMEM_LIMIT),
        name="diffattn",
    )(qk, qk, v, lq1, lk1, lq2, lk2, sub_gain)


def _outproj_kernel(h_ref, f_ref, a_ref, sg_ref, w_ref, o_ref):
    o_ref[...] = _outproj_update(h_ref, f_ref, a_ref, sg_ref, w_ref)


def _outproj(h, f_out, a_out, sg, w_bf, *, layer):
    rows, d = h.shape
    tm = ROW_TILE
    row_spec = lambda w: pl.BlockSpec((tm, w), lambda i: (i, 0))
    return pl.pallas_call(
        _outproj_kernel,
        grid=(rows // tm,),
        in_specs=[row_spec(d), row_spec(f_out.shape[1]), row_spec(a_out.shape[1]),
                  row_spec(sg.shape[1]), _layer_spec(w_bf, layer)],
        out_specs=row_spec(d),
        out_shape=jax.ShapeDtypeStruct((rows, d), F32),
        input_output_aliases={0: 0},
        compiler_params=pltpu.CompilerParams(
            dimension_semantics=("arbitrary",), vmem_limit_bytes=VMEM_LIMIT),
        name="outproj",
    )(h, f_out, a_out, sg, w_bf)


def _assemble_kernel(direct_ref, lo_ref, hi_ref, o_ref, *, anti_diag, direct_end, rev_lo, rev_hi):
    r = o_ref.shape[1]
    t = pl.program_id(1)
    a = t * r + lax.broadcasted_iota(jnp.int32, (r, 1), 0)
    direct = jnp.where(a < direct_end, direct_ref[0], 0.0)

    @pl.when(t < rev_lo // r)
    def _copy():
        o_ref[0] = direct

    @pl.when(t >= rev_lo // r)
    def _reverse():
        src = jnp.concatenate([lo_ref[0], hi_ref[0]], axis=0)
        ri = lax.broadcasted_iota(jnp.int32, (r, 2 * r), 0)
        ci = lax.broadcasted_iota(jnp.int32, (r, 2 * r), 1)
        perm = jnp.where(ri + ci == anti_diag, 1.0, 0.0).astype(BF16)
        p0 = src.astype(BF16)
        r1 = src - p0.astype(F32)
        p1 = r1.astype(BF16)
        p2 = (r1 - p1.astype(F32)).astype(BF16)
        rev = (jnp.dot(perm, p0, preferred_element_type=F32)
               + jnp.dot(perm, p1, preferred_element_type=F32)
               + jnp.dot(perm, p2, preferred_element_type=F32))
        in_rev = jnp.abs(2 * a - (rev_lo + rev_hi - 1)) <= (rev_hi - rev_lo - 1)
        o_ref[0] = jnp.where(in_rev, rev, direct)


def _assemble(direct, src, *, n_rows, direct_end, rev, total):
    b, _, d = direct.shape
    r = ASSEMBLE_TILE
    q0, rem = divmod(total - (r - 1), r)
    n_direct_tiles = -(-direct_end // r)
    n_src_tiles = src.shape[1] // r
    rev_tile = lambda u: jnp.clip(u, 0, n_src_tiles - 1)
    return pl.pallas_call(
        functools.partial(_assemble_kernel, anti_diag=r - 1 + rem, direct_end=direct_end,
                          rev_lo=rev[0], rev_hi=rev[1]),
        grid=(b, n_rows // r),
        in_specs=[pl.BlockSpec((1, r, d), lambda bi, t: (bi, jnp.minimum(t, n_direct_tiles - 1), 0)),
                  pl.BlockSpec((1, r, d), lambda bi, t: (bi, rev_tile(q0 - t), 0)),
                  pl.BlockSpec((1, r, d), lambda bi, t: (bi, rev_tile(q0 - t + 1), 0))],
        out_specs=pl.BlockSpec((1, r, d), lambda bi, t: (bi, t, 0)),
        out_shape=jax.ShapeDtypeStruct((b, n_rows, d), direct.dtype),
        compiler_params=pltpu.CompilerParams(
            dimension_semantics=("arbitrary", "arbitrary"), vmem_limit_bytes=VMEM_LIMIT),
        name="assemble",
    )(direct, src, src)


def _row_positions(l, half):
    n_tok = l // 2 + 1 - N_META
    j = jnp.arange(half, dtype=jnp.int32)
    p = jnp.where(j < n_tok, j + N_META, n_tok + N_META - 1 - j)
    valid_a = j < n_tok + N_META
    valid_b = valid_a & (p != 0) & (2 * p != l)
    pos = jnp.concatenate([p, l - p])
    return jnp.where(jnp.concatenate([valid_a, valid_b]), pos, 0)


def _rope_tables(pos, hd):
    rot = hd // 4
    half = rot // 2
    inv_freq = ROPE_THETA ** (-jnp.arange(0, rot, 2, dtype=F32) / rot)
    ang = pos.astype(F32)[:, None] * inv_freq[None, :]
    cos, sin = jnp.cos(ang), jnp.sin(ang)
    d = jnp.arange(LANES) % hd
    first = d < half
    second = (d >= half) & (d < rot)
    idx = jnp.where(second, d - half, jnp.where(first, d, 0))
    cos_l, sin_l = cos[:, idx], sin[:, idx]
    ct = jnp.where(first | second, cos_l, 1.0)
    sa = jnp.where(first, -sin_l, 0.0)
    sb = jnp.where(second, sin_l, 0.0)
    return ct, sa, sb


def _position_dft(pos, l, half):
    blk = 64
    n_tok = l // 2 + 1 - N_META
    p = pos[:half]
    valid = jnp.arange(half) < n_tok + N_META
    def trig(k):
        ang = ((k[:, None] * p[None, :]) % l).astype(F32) * (2.0 * math.pi / l)
        return jnp.cos(ang), jnp.sin(ang)
    ch, sh = trig(jnp.arange(half // blk, dtype=jnp.int32) * blk)
    cl, sl = trig(jnp.arange(blk, dtype=jnp.int32) + N_META)
    cm = (ch[:, None, :] * cl[None, :, :] - sh[:, None, :] * sl[None, :, :]).reshape(half, half)
    sm = (sh[:, None, :] * cl[None, :, :] + ch[:, None, :] * sl[None, :, :]).reshape(half, half)
    cmeta, smeta = trig(p[n_tok:n_tok + N_META])
    cm = lax.dynamic_update_slice(cm, cmeta, (n_tok, 0))
    sm = lax.dynamic_update_slice(sm, smeta, (n_tok, 0))
    ok = valid[:, None] & valid[None, :]
    return jnp.where(ok, cm, 0.0).astype(BF16), jnp.where(ok, -sm, 0.0).astype(BF16)


def _channel_dft(n):
    c = jnp.arange(n, dtype=jnp.int32)
    ang = ((c[:, None] * c[None, :]) % n).astype(F32) * (2.0 * math.pi / n)
    return jnp.concatenate([jnp.cos(ang), jnp.sin(ang)], axis=1).astype(BF16)


def kernel(x, meta_tokens, norm_gain, w_in, w_fourier, q_norm_gain, k_norm_gain,
           lambda_q1, lambda_k1, lambda_q2, lambda_k2, subln_gain, w_out):
    b, seq, d = x.shape
    depth = w_in.shape[0]
    hd = q_norm_gain.shape[1]
    vd = subln_gain.shape[1]
    fw = w_fourier.shape[1] * w_fourier.shape[2]
    gd = w_fourier.shape[2]
    qkw = N_ATTN_HEADS * 2 * hd
    vw = N_ATTN_HEADS * vd
    l = seq + N_META
    half = _half_len(l)
    l_pad = 2 * half
    n_first = l // 2 + 1
    n_tok = n_first - N_META
    n_pair = n_tok - 1
    assert l % 2 == 0 and N_META < n_first
    assert seq % ASSEMBLE_TILE == 0 and l_pad % ASSEMBLE_TILE == 0
    assert 2 * hd == LANES and vd == LANES and gd == LANES
    assert l_pad % ROW_TILE == 0 and l_pad % FUSED_ROW_TILE == 0
    assert ROW_TILE % ROW_SUBTILE == 0 and FUSED_ROW_TILE % ROW_SUBTILE == 0
    assert half % DFT_TILE == 0 and l_pad % Q_TILE == 0
    assert half % 64 == 0
    valid_ranges = ((0, n_first), (half, half + n_pair), (half + n_tok, half + n_first - 1))

    pos = _row_positions(l, half)
    ct, sa, sb = _rope_tables(pos, hd)
    cmat, smat = _position_dft(pos, l, half)
    cdft = _channel_dft(gd)
    dft_scale = 1.0 / math.sqrt(l * gd)
    qscale = (hd ** -0.5) * math.log2(math.e)

    h = _assemble(x, x, n_rows=l_pad, direct_end=n_tok, rev=(half, half + n_pair),
                  total=half + seq - N_META)
    meta = jnp.broadcast_to(meta_tokens[::-1][None].astype(x.dtype), (b, N_META, d))
    h = lax.dynamic_update_slice(h, meta, (0, n_tok, 0))
    h = lax.dynamic_update_slice(h, x[:, seq - N_META + 1:], (0, half + n_tok, 0))
    h = h.reshape(b * l_pad, d)

    w_in_bf = w_in.astype(BF16)
    w_out_bf = w_out.astype(BF16)
    w_f_bf = w_fourier.astype(BF16)

    prev = None
    for li in range(depth):
        lambda_init = 0.8 - 0.6 * math.exp(-0.3 * li)
        qk_gain = jnp.concatenate([jnp.tile(q_norm_gain[li], qkw // hd),
                                   jnp.tile(k_norm_gain[li], qkw // hd)])[None]
        outs = _inproj(h, norm_gain[li][None], w_in_bf, qk_gain, ct, sa, sb, cdft,
                       layer=li, l_pad=l_pad, fw=fw, qkw=qkw, vw=vw, hd=hd, qscale=qscale, prev=prev)
        if prev is not None:
            h, outs = outs[0], outs[1:]
        pc, ps, qk, v, sg = outs
        f_out = _fourier(cmat, smat, pc.reshape(b, 2, half, fw), ps.reshape(b, 2, half, fw),
                         w_f_bf[li], scale=dft_scale)
        score_bound = (jnp.max(jnp.abs(q_norm_gain[li])) * jnp.max(jnp.abs(k_norm_gain[li]))
                       * (hd * qscale * 1.02))
        attend = lambda subtract_max: functools.partial(
            _attention, hd=hd, valid_ranges=valid_ranges, lambda_init=lambda_init,
            subtract_max=subtract_max)
        a_out = lax.cond(score_bound <= MAX_UNSHIFTED_SCORE, attend(False), attend(True),
                         qk.reshape(b, l_pad, 2 * qkw), v.reshape(b, l_pad, vw),
                         lambda_q1[li][None], lambda_k1[li][None],
                         lambda_q2[li][None], lambda_k2[li][None], subln_gain[li][None])
        prev = (f_out.reshape(b * l_pad, fw), a_out.reshape(b * l_pad, vw), sg, w_out_bf)

    h = _outproj(h, *prev, layer=depth - 1).reshape(b, l_pad, d)
    out = _assemble(h, h, n_rows=seq, direct_end=n_tok, rev=(n_tok, seq - N_META + 1),
                    total=half + seq - N_META)
    return lax.dynamic_update_slice(out, h[:, half + n_tok:half + n_first - 1],
                                    (0, seq - N_META + 1, 0))
```

```python
import functools
import math

import jax
import jax.numpy as jnp
from jax import lax
from jax.experimental import pallas as pl
from jax.experimental.pallas import tpu as pltpu

F32 = jnp.float32
BF16 = jnp.bfloat16

N_META = 16
N_FOURIER_GROUPS = 4
N_ATTN_HEADS = 4
ROPE_THETA = 500000.0
NORM_EPS = 1e-6

MXU_TILE = 256
LANES = 128
VMEM_LIMIT = 56 * 1024 * 1024

ROW_TILE = 1088
FUSED_ROW_TILE = 544
ROW_SUBTILE = 272
DFT_TILE = 1088
Q_TILE = 256
HEADS_PER_STEP = 4
ASSEMBLE_TILE = 256
MAX_UNSHIFTED_SCORE = 64.0


def _half_len(l):
    return -(-(l // 2 + 1) // LANES) * LANES


def _outproj_update(h_ref, f_ref, a_ref, sg_ref, w_ref):
    fw = f_ref.shape[1]
    sg = sg_ref[...]
    yf = f_ref[...] * sg[:, :fw]
    ya = a_ref[...] * sg[:, fw:]
    return (h_ref[...]
            + jnp.dot(yf, w_ref[:fw, :], preferred_element_type=F32)
            + jnp.dot(ya, w_ref[fw:, :], preferred_element_type=F32))


def _row_subtiles(n_rows):
    return [pl.ds(i * ROW_SUBTILE, ROW_SUBTILE) for i in range(n_rows // ROW_SUBTILE)]


def _inproj_kernel(h_ref, g_ref, w_ref, qkg_ref, ct_ref, sa_ref, sb_ref, cdft_ref, *outs, **kw):
    for sub in _row_subtiles(h_ref.shape[0]):
        _inproj_body(h_ref[sub], g_ref, w_ref, qkg_ref, ct_ref.at[sub], sa_ref.at[sub],
                     sb_ref.at[sub], cdft_ref, *[o.at[sub] for o in outs], **kw)


def _outproj_inproj_kernel(h_ref, f_ref, a_ref, sgp_ref, wo_ref, g_ref, w_ref, qkg_ref,
                           ct_ref, sa_ref, sb_ref, cdft_ref, hout_ref, *outs, **kw):
    for sub in _row_subtiles(h_ref.shape[0]):
        h = _outproj_update(h_ref.at[sub], f_ref.at[sub], a_ref.at[sub], sgp_ref.at[sub], wo_ref)
        hout_ref[sub] = h
        _inproj_body(h, g_ref, w_ref, qkg_ref, ct_ref.at[sub], sa_ref.at[sub],
                     sb_ref.at[sub], cdft_ref, *[o.at[sub] for o in outs], **kw)


def _inproj_body(x, g_ref, w_ref, qkg_ref, ct_ref, sa_ref, sb_ref, cdft_ref,
                 pc_ref, ps_ref, qk_ref, v_ref, sg_ref, *, fw, qkw, vw, hd, qscale):
    ms = jnp.mean(x * x, axis=-1, keepdims=True)
    hn = ((x * lax.rsqrt(ms + NORM_EPS)) * g_ref[...]).astype(BF16)

    f = jnp.dot(hn, w_ref[:, 0:fw], preferred_element_type=F32)
    gd = fw // N_FOURIER_GROUPS
    for g in range(N_FOURIER_GROUPS):
        pg = jnp.dot(f[:, g * gd:(g + 1) * gd].astype(BF16), cdft_ref[...],
                     preferred_element_type=F32)
        pc_ref[:, g * gd:(g + 1) * gd] = pg[:, :gd].astype(BF16)
        ps_ref[:, g * gd:(g + 1) * gd] = pg[:, gd:].astype(BF16)

    qk = jnp.dot(hn, w_ref[:, fw:fw + 2 * qkw], preferred_element_type=F32)
    tm = qk.shape[0]
    lo_mask = lax.broadcasted_iota(jnp.int32, (tm, LANES), 1) < hd
    ct = ct_ref[...]
    sa = sa_ref[...]
    sb = sb_ref[...]
    for c in range(2 * qkw // LANES):
        xc = qk[:, c * LANES:(c + 1) * LANES]
        x2 = xc * xc
        lo = jnp.sum(jnp.where(lo_mask, x2, 0.0), axis=-1, keepdims=True)
        hi = jnp.sum(jnp.where(lo_mask, 0.0, x2), axis=-1, keepdims=True)
        msc = jnp.where(lo_mask, lo, hi) * (1.0 / hd)
        y = (xc * lax.rsqrt(msc + NORM_EPS)) * qkg_ref[:, c * LANES:(c + 1) * LANES]
        yr = (y * ct + pltpu.roll(y, LANES - 8, 1) * sa + pltpu.roll(y, 8, 1) * sb)
        if c < qkw // LANES:
            yr = yr * qscale
        qk_ref[:, c * LANES:(c + 1) * LANES] = yr.astype(BF16)

    v = jnp.dot(hn, w_ref[:, fw + 2 * qkw:fw + 2 * qkw + vw], preferred_element_type=F32)
    v_ref[...] = v.astype(BF16)

    gate = jnp.dot(hn, w_ref[:, fw + 2 * qkw + vw:], preferred_element_type=F32)
    sg_ref[...] = (gate * (1.0 / (1.0 + jnp.exp(-gate)))).astype(sg_ref.dtype)


def _layer_spec(w, layer):
    return pl.BlockSpec((None,) + w.shape[1:], lambda i: (layer,) + (0,) * (w.ndim - 1),
                        pipeline_mode=pl.Buffered(1))


def _inproj(h, gain, w_bf, qk_gain, ct, sa, sb, cdft, *, layer, l_pad, fw, qkw, vw, hd, qscale,
            prev=None):
    rows, d = h.shape
    tm = ROW_TILE if prev is None else FUSED_ROW_TILE
    n_pos_tiles = l_pad // tm
    mixw = w_bf.shape[2] - (fw + 2 * qkw + vw)
    row_spec = lambda w: pl.BlockSpec((tm, w), lambda i: (i, 0))
    const = lambda shape: pl.BlockSpec(shape, lambda i: (0,) * len(shape))
    pos_spec = pl.BlockSpec((tm, LANES), lambda i: (i % n_pos_tiles, 0))
    kw = dict(fw=fw, qkw=qkw, vw=vw, hd=hd, qscale=qscale)
    args = [h, gain, w_bf, qk_gain, ct, sa, sb, cdft]
    in_specs = [row_spec(d), const((1, d)), _layer_spec(w_bf, layer), const((1, 2 * qkw)),
                pos_spec, pos_spec, pos_spec, const(cdft.shape)]
    out_specs = [row_spec(fw), row_spec(fw), row_spec(2 * qkw), row_spec(vw), row_spec(mixw)]
    out_shape = [jax.ShapeDtypeStruct((rows, fw), BF16),
                 jax.ShapeDtypeStruct((rows, fw), BF16),
                 jax.ShapeDtypeStruct((rows, 2 * qkw), BF16),
                 jax.ShapeDtypeStruct((rows, vw), BF16),
                 jax.ShapeDtypeStruct((rows, mixw), BF16)]
    if prev is None:
        body, name, aliases = functools.partial(_inproj_kernel, **kw), "inproj", {}
    else:
        f_out, a_out, sg_prev, wo_bf = prev
        body, name, aliases = functools.partial(_outproj_inproj_kernel, **kw), "outproj_inproj", {0: 0}
        args[1:1] = [f_out, a_out, sg_prev, wo_bf]
        in_specs[1:1] = [row_spec(f_out.shape[1]), row_spec(a_out.shape[1]),
                         row_spec(sg_prev.shape[1]), _layer_spec(wo_bf, layer - 1)]
        out_specs.insert(0, row_spec(d))
        out_shape.insert(0, jax.ShapeDtypeStruct((rows, d), F32))
    return pl.pallas_call(
        body,
        grid=(rows // tm,),
        in_specs=in_specs,
        out_specs=out_specs,
        out_shape=out_shape,
        input_output_aliases=aliases,
        compiler_params=pltpu.CompilerParams(
            dimension_semantics=("arbitrary",), vmem_limit_bytes=VMEM_LIMIT),
        name=name,
    )(*args)


def _fourier_kernel(cm_ref, sm_ref, pc_ref, ps_ref, wf_ref, o_ref, pcf_ref, psf_ref, *, scale):
    @pl.when(pl.program_id(1) == 0)
    def _fold():
        pcf_ref[...] = pc_ref[0, 0] + pc_ref[0, 1]
        psf_ref[...] = ps_ref[0, 0] - ps_ref[0, 1]

    even = jnp.dot(cm_ref[...], pcf_ref[...], preferred_element_type=F32) * scale
    odd = jnp.dot(sm_ref[...], psf_ref[...], preferred_element_type=F32) * scale
    gd = even.shape[1] // N_FOURIER_GROUPS
    for half, re in enumerate((even + odd, even - odd)):
        for g in range(N_FOURIER_GROUPS):
            o_ref[0, half, :, g * gd:(g + 1) * gd] = jnp.dot(
                re[:, g * gd:(g + 1) * gd].astype(BF16), wf_ref[g],
                preferred_element_type=F32).astype(o_ref.dtype)


def _fourier(cmat, smat, pc, ps, wf_bf, *, scale):
    b, _, half, fw = pc.shape
    tk = DFT_TILE
    return pl.pallas_call(
        functools.partial(_fourier_kernel, scale=scale),
        grid=(b, half // tk),
        in_specs=[pl.BlockSpec((tk, half), lambda j, i: (i, 0)),
                  pl.BlockSpec((tk, half), lambda j, i: (i, 0)),
                  pl.BlockSpec((1, 2, half, fw), lambda j, i: (j, 0, 0, 0)),
                  pl.BlockSpec((1, 2, half, fw), lambda j, i: (j, 0, 0, 0)),
                  pl.BlockSpec(wf_bf.shape, lambda j, i: (0, 0, 0))],
        out_specs=pl.BlockSpec((1, 2, tk, fw), lambda j, i: (j, 0, i, 0)),
        out_shape=jax.ShapeDtypeStruct((b, 2, half, fw), BF16),
        scratch_shapes=[pltpu.VMEM((half, fw), BF16), pltpu.VMEM((half, fw), BF16)],
        compiler_params=pltpu.CompilerParams(
            dimension_semantics=("arbitrary", "arbitrary"), vmem_limit_bytes=VMEM_LIMIT),
        name="fourier",
    )(cmat, smat, pc, ps, wf_bf)


KEY_ALIGN = 16


def _key_plan(valid_ranges):
    covers = []
    for a, b in sorted(valid_ranges):
        lo, hi = a // KEY_ALIGN * KEY_ALIGN, -(-b // KEY_ALIGN) * KEY_ALIGN
        if covers and lo <= covers[-1][1]:
            covers[-1][1] = max(covers[-1][1], hi)
        else:
            covers.append([lo, hi])
    pieces, tail_rows = [], []
    for lo, hi in covers:
        n_full = (hi - lo) // MXU_TILE
        for t in range(n_full):
            a, b = lo + t * MXU_TILE, lo + (t + 1) * MXU_TILE
            pads = [(x - a, y - a) for x, y in _padding_ranges(a, b, valid_ranges)]
            if pieces and not pads and not pieces[-1][1] and pieces[-1][0][-1][1] == a:
                pieces[-1][0][-1] = (pieces[-1][0][-1][0], b)
            else:
                pieces.append(([(a, b)], pads))
        if lo + n_full * MXU_TILE < hi:
            tail_rows.append((lo + n_full * MXU_TILE, hi))
    if tail_rows:
        pads, off = [], 0
        for a, b in tail_rows:
            pads += [(x - a + off, y - a + off) for x, y in _padding_ranges(a, b, valid_ranges)]
            off += b - a
        pieces.insert(0, (tail_rows, pads))
    return pieces


def _rows(x, row_ranges):
    parts = [x[a:b] for a, b in row_ranges]
    return parts[0] if len(parts) == 1 else jnp.concatenate(parts, axis=0)


def _padding_ranges(lo, hi, valid_ranges):
    out, cur = [], lo
    for a, b in sorted(valid_ranges):
        a, b = max(a, lo), min(b, hi)
        if a >= b:
            continue
        if a > cur:
            out.append((cur, a))
        cur = max(cur, b)
    if cur < hi:
        out.append((cur, hi))
    return out


def _attend_head(q, k, v, lam, sub_gain, *, hd, valid_ranges, lambda_init, subtract_max):
    tq = q.shape[0]
    lane = lax.broadcasted_iota(jnp.int32, q.shape, 1)
    zero = jnp.zeros_like(q)
    contract_last = (((1,), (1,)), ((), ()))
    contract_first = (((0,), (0,)), ((), ()))

    q2 = jnp.concatenate([jnp.where(lane < hd, q, zero), jnp.where(lane < hd, zero, q)], axis=0)
    plan = _key_plan(valid_ranges)
    parts = []
    for row_ranges, pads in plan:
        sp = lax.dot_general(_rows(k, row_ranges), q2, contract_last,
                             preferred_element_type=F32)
        if pads:
            row2 = 2 * lax.broadcasted_iota(jnp.int32, sp.shape, 0)
            for a, b in pads:
                w1 = b - a - 1
                sp = jnp.where(jnp.abs(row2 - (2 * a + w1)) <= w1, -jnp.inf, sp)
        parts.append(sp)
    if subtract_max:
        m = functools.reduce(jnp.maximum, [jnp.max(sp, axis=0, keepdims=True) for sp in parts])
        parts = [sp - m for sp in parts]
    parts = [jnp.exp2(sp) for sp in parts]
    l = functools.reduce(jnp.add, [jnp.sum(p, axis=0, keepdims=True) for p in parts])
    o_t = functools.reduce(jnp.add, [
        lax.dot_general(_rows(v, row_ranges), p.astype(BF16), contract_first,
                        preferred_element_type=F32)
        for (row_ranges, _), p in zip(plan, parts)]) * (1.0 / l)
    o = (o_t[:, :tq] - lam * o_t[:, tq:]).T
    ms = jnp.mean(o * o, axis=-1, keepdims=True)
    return ((o * lax.rsqrt(ms + NORM_EPS)) * sub_gain) * (1.0 - lambda_init)


def _attn_kernel(q_ref, k_ref, v_ref, lq1_ref, lk1_ref, lq2_ref, lk2_ref, sub_ref, o_ref,
                 *, hd, lambda_init, **kw):
    lam = (jnp.exp(jnp.sum(lq1_ref[...] * lk1_ref[...], axis=-1, keepdims=True))
           - jnp.exp(jnp.sum(lq2_ref[...] * lk2_ref[...], axis=-1, keepdims=True))
           + lambda_init)
    w = 2 * hd
    for i in range(q_ref.shape[2] // w):
        cols = slice(i * w, (i + 1) * w)
        o_ref[0, :, cols] = _attend_head(
            q_ref[0, :, cols], k_ref[0, :, cols], v_ref[0, :, cols], lam, sub_ref[...],
            hd=hd, lambda_init=lambda_init, **kw).astype(o_ref.dtype)


def _attention(qk, v, lq1, lk1, lq2, lk2, sub_gain, *, hd, valid_ranges, lambda_init,
               subtract_max):
    b, l_pad, _ = qk.shape
    vd = 2 * hd
    tq = Q_TILE
    hp = HEADS_PER_STEP
    n_groups = N_ATTN_HEADS // hp
    vec = lambda n: pl.BlockSpec((1, n), lambda bi, h, qi: (0, 0))
    return pl.pallas_call(
        functools.partial(_attn_kernel, hd=hd, valid_ranges=valid_ranges,
                          lambda_init=lambda_init, subtract_max=subtract_max),
        grid=(b, n_groups, l_pad // tq),
        in_specs=[pl.BlockSpec((1, tq, hp * 2 * hd), lambda bi, h, qi: (bi, qi, h)),
                  pl.BlockSpec((1, l_pad, hp * 2 * hd), lambda bi, h, qi: (bi, 0, n_groups + h)),
                  pl.BlockSpec((1, l_pad, hp * vd), lambda bi, h, qi: (bi, 0, h)),
                  vec(hd), vec(hd), vec(hd), vec(hd), vec(vd)],
        out_specs=pl.BlockSpec((1, tq, hp * vd), lambda bi, h, qi: (bi, qi, h)),
        out_shape=jax.ShapeDtypeStruct((b, l_pad, N_ATTN_HEADS * vd), BF16),
        compiler_params=pltpu.CompilerParams(
            dimension_semantics=("arbitrary", "arbitrary", "arbitrary"),
            vmem_limit_bytes=VMEM_LIMIT),
        name="diffattn",
    )(qk, qk, v, lq1, lk1, lq2, lk2, sub_gain)


def _outproj_kernel(h_ref, f_ref, a_ref, sg_ref, w_ref, o_ref):
    o_ref[...] = _outproj_update(h_ref, f_ref, a_ref, sg_ref, w_ref)


def _outproj(h, f_out, a_out, sg, w_bf, *, layer):
    rows, d = h.shape
    tm = ROW_TILE
    row_spec = lambda w: pl.BlockSpec((tm, w), lambda i: (i, 0))
    return pl.pallas_call(
        _outproj_kernel,
        grid=(rows // tm,),
        in_specs=[row_spec(d), row_spec(f_out.shape[1]), row_spec(a_out.shape[1]),
                  row_spec(sg.shape[1]), _layer_spec(w_bf, layer)],
        out_specs=row_spec(d),
        out_shape=jax.ShapeDtypeStruct((rows, d), F32),
        input_output_aliases={0: 0},
        compiler_params=pltpu.CompilerParams(
            dimension_semantics=("arbitrary",), vmem_limit_bytes=VMEM_LIMIT),
        name="outproj",
    )(h, f_out, a_out, sg, w_bf)


def _assemble_kernel(direct_ref, lo_ref, hi_ref, o_ref, *, anti_diag, direct_end, rev_lo, rev_hi):
    r = o_ref.shape[1]
    t = pl.program_id(1)
    a = t * r + lax.broadcasted_iota(jnp.int32, (r, 1), 0)
    direct = jnp.where(a < direct_end, direct_ref[0], 0.0)

    @pl.when(t < rev_lo // r)
    def _copy():
        o_ref[0] = direct

    @pl.when(t >= rev_lo // r)
    def _reverse():
        src = jnp.concatenate([lo_ref[0], hi_ref[0]], axis=0)
        ri = lax.broadcasted_iota(jnp.int32, (r, 2 * r), 0)
        ci = lax.broadcasted_iota(jnp.int32, (r, 2 * r), 1)
        perm = jnp.where(ri + ci == anti_diag, 1.0, 0.0).astype(BF16)
        p0 = src.astype(BF16)
        r1 = src - p0.astype(F32)
        p1 = r1.astype(BF16)
        p2 = (r1 - p1.astype(F32)).astype(BF16)
        rev = (jnp.dot(perm, p0, preferred_element_type=F32)
               + jnp.dot(perm, p1, preferred_element_type=F32)
               + jnp.dot(perm, p2, preferred_element_type=F32))
        in_rev = jnp.abs(2 * a - (rev_lo + rev_hi - 1)) <= (rev_hi - rev_lo - 1)
        o_ref[0] = jnp.where(in_rev, rev, direct)


def _assemble(direct, src, *, n_rows, direct_end, rev, total):
    b, _, d = direct.shape
    r = ASSEMBLE_TILE
    q0, rem = divmod(total - (r - 1), r)
    n_direct_tiles = -(-direct_end // r)
    n_src_tiles = src.shape[1] // r
    rev_tile = lambda u: jnp.clip(u, 0, n_src_tiles - 1)
    return pl.pallas_call(
        functools.partial(_assemble_kernel, anti_diag=r - 1 + rem, direct_end=direct_end,
                          rev_lo=rev[0], rev_hi=rev[1]),
        grid=(b, n_rows // r),
        in_specs=[pl.BlockSpec((1, r, d), lambda bi, t: (bi, jnp.minimum(t, n_direct_tiles - 1), 0)),
                  pl.BlockSpec((1, r, d), lambda bi, t: (bi, rev_tile(q0 - t), 0)),
                  pl.BlockSpec((1, r, d), lambda bi, t: (bi, rev_tile(q0 - t + 1), 0))],
        out_specs=pl.BlockSpec((1, r, d), lambda bi, t: (bi, t, 0)),
        out_shape=jax.ShapeDtypeStruct((b, n_rows, d), direct.dtype),
        compiler_params=pltpu.CompilerParams(
            dimension_semantics=("arbitrary", "arbitrary"), vmem_limit_bytes=VMEM_LIMIT),
        name="assemble",
    )(direct, src, src)


def _row_positions(l, half):
    n_tok = l // 2 + 1 - N_META
    j = jnp.arange(half, dtype=jnp.int32)
    p = jnp.where(j < n_tok, j + N_META, n_tok + N_META - 1 - j)
    valid_a = j < n_tok + N_META
    valid_b = valid_a & (p != 0) & (2 * p != l)
    pos = jnp.concatenate([p, l - p])
    return jnp.where(jnp.concatenate([valid_a, valid_b]), pos, 0)


def _rope_tables(pos, hd):
    rot = hd // 4
    half = rot // 2
    inv_freq = ROPE_THETA ** (-jnp.arange(0, rot, 2, dtype=F32) / rot)
    ang = pos.astype(F32)[:, None] * inv_freq[None, :]
    cos, sin = jnp.cos(ang), jnp.sin(ang)
    d = jnp.arange(LANES) % hd
    first = d < half
    second = (d >= half) & (d < rot)
    idx = jnp.where(second, d - half, jnp.where(first, d, 0))
    cos_l, sin_l = cos[:, idx], sin[:, idx]
    ct = jnp.where(first | second, cos_l, 1.0)
    sa = jnp.where(first, -sin_l, 0.0)
    sb = jnp.where(second, sin_l, 0.0)
    return ct, sa, sb


def _position_dft(pos, l, half):
    blk = 64
    n_tok = l // 2 + 1 - N_META
    p = pos[:half]
    valid = jnp.arange(half) < n_tok + N_META
    def trig(k):
        ang = ((k[:, None] * p[None, :]) % l).astype(F32) * (2.0 * math.pi / l)
        return jnp.cos(ang), jnp.sin(ang)
    ch, sh = trig(jnp.arange(half // blk, dtype=jnp.int32) * blk)
    cl, sl = trig(jnp.arange(blk, dtype=jnp.int32) + N_META)
    cm = (ch[:, None, :] * cl[None, :, :] - sh[:, None, :] * sl[None, :, :]).reshape(half, half)
    sm = (sh[:, None, :] * cl[None, :, :] + ch[:, None, :] * sl[None, :, :]).reshape(half, half)
    cmeta, smeta = trig(p[n_tok:n_tok + N_META])
    cm = lax.dynamic_update_slice(cm, cmeta, (n_tok, 0))
    sm = lax.dynamic_update_slice(sm, smeta, (n_tok, 0))
    ok = valid[:, None] & valid[None, :]
    return jnp.where(ok, cm, 0.0).astype(BF16), jnp.where(ok, -sm, 0.0).astype(BF16)


def _channel_dft(n):
    c = jnp.arange(n, dtype=jnp.int32)
    ang = ((c[:, None] * c[None, :]) % n).astype(F32) * (2.0 * math.pi / n)
    return jnp.concatenate([jnp.cos(ang), jnp.sin(ang)], axis=1).astype(BF16)


def kernel(x, meta_tokens, norm_gain, w_in, w_fourier, q_norm_gain, k_norm_gain,
           lambda_q1, lambda_k1, lambda_q2, lambda_k2, subln_gain, w_out):
    b, seq, d = x.shape
    depth = w_in.shape[0]
    hd = q_norm_gain.shape[1]
    vd = subln_gain.shape[1]
    fw = w_fourier.shape[1] * w_fourier.shape[2]
    gd = w_fourier.shape[2]
    qkw = N_ATTN_HEADS * 2 * hd
    vw = N_ATTN_HEADS * vd
    l = seq + N_META
    half = _half_len(l)
    l_pad = 2 * half
    n_first = l // 2 + 1
    n_tok = n_first - N_META
    n_pair = n_tok - 1
    assert l % 2 == 0 and N_META < n_first
    assert seq % ASSEMBLE_TILE == 0 and l_pad % ASSEMBLE_TILE == 0
    assert 2 * hd == LANES and vd == LANES and gd == LANES
    assert l_pad % ROW_TILE == 0 and l_pad % FUSED_ROW_TILE == 0
    assert ROW_TILE % ROW_SUBTILE == 0 and FUSED_ROW_TILE % ROW_SUBTILE == 0
    assert half % DFT_TILE == 0 and l_pad % Q_TILE == 0
    assert half % 64 == 0
    valid_ranges = ((0, n_first), (half, half + n_pair), (half + n_tok, half + n_first - 1))

    pos = _row_positions(l, half)
    ct, sa, sb = _rope_tables(pos, hd)
    cmat, smat = _position_dft(pos, l, half)
    cdft = _channel_dft(gd)
    dft_scale = 1.0 / math.sqrt(l * gd)
    qscale = (hd ** -0.5) * math.log2(math.e)

    h = _assemble(x, x, n_rows=l_pad, direct_end=n_tok, rev=(half, half + n_pair),
                  total=half + seq - N_META)
    meta = jnp.broadcast_to(meta_tokens[::-1][None].astype(x.dtype), (b, N_META, d))
    h = lax.dynamic_update_slice(h, meta, (0, n_tok, 0))
    h = lax.dynamic_update_slice(h, x[:, seq - N_META + 1:], (0, half + n_tok, 0))
    h = h.reshape(b * l_pad, d)

    w_in_bf = w_in.astype(BF16)
    w_out_bf = w_out.astype(BF16)
    w_f_bf = w_fourier.astype(BF16)

    prev = None
    for li in range(depth):
        lambda_init = 0.8 - 0.6 * math.exp(-0.3 * li)
        qk_gain = jnp.concatenate([jnp.tile(q_norm_gain[li], qkw // hd),
                                   jnp.tile(k_norm_gain[li], qkw // hd)])[None]
        outs = _inproj(h, norm_gain[li][None], w_in_bf, qk_gain, ct, sa, sb, cdft,
                       layer=li, l_pad=l_pad, fw=fw, qkw=qkw, vw=vw, hd=hd, qscale=qscale, prev=prev)
        if prev is not None:
            h, outs = outs[0], outs[1:]
        pc, ps, qk, v, sg = outs
        f_out = _fourier(cmat, smat, pc.reshape(b, 2, half, fw), ps.reshape(b, 2, half, fw),
                         w_f_bf[li], scale=dft_scale)
        score_bound = (jnp.max(jnp.abs(q_norm_gain[li])) * jnp.max(jnp.abs(k_norm_gain[li]))
                       * (hd * qscale * 1.02))
        attend = lambda subtract_max: functools.partial(
            _attention, hd=hd, valid_ranges=valid_ranges, lambda_init=lambda_init,
            subtract_max=subtract_max)
        a_out = lax.cond(score_bound <= MAX_UNSHIFTED_SCORE, attend(False), attend(True),
                         qk.reshape(b, l_pad, 2 * qkw), v.reshape(b, l_pad, vw),
                         lambda_q1[li][None], lambda_k1[li][None],
                         lambda_q2[li][None], lambda_k2[li][None], subln_gain[li][None])
        prev = (f_out.reshape(b * l_pad, fw), a_out.reshape(b * l_pad, vw), sg, w_out_bf)

    h = _outproj(h, *prev, layer=depth - 1).reshape(b, l_pad, d)
    out = _assemble(h, h, n_rows=seq, direct_end=n_tok, rev=(n_tok, seq - N_META + 1),
                    total=half + seq - N_META)
    return lax.dynamic_update_slice(out, h[:, half + n_tok:half + n_first - 1],
                                    (0, seq - N_META + 1, 0))
```
